```python
import jax, jax.numpy as jnp
from jax import lax
import numpy as np

D_MODEL = 1024
BATCH = 32
SEQ = 256
DEPTH = 4
DEC_BATCH = 4
DEC_SEQ = 4096
PAST_LEN = 256

GRID_W = 64
MIX_WIDTH = D_MODEL
GROUP_W = MIX_WIDTH // 4
A_GROUPS = 4
A_CH = GROUP_W // A_GROUPS
CHUNK = 128
B_WIDTH = GROUP_W
CONV_WIDTH = 3
C_GROUPS = 4
C_CH = GROUP_W // C_GROUPS
MLA_HEADS = 4
QK_NOPE = 64
QK_ROPE = 32
V_DIM = 64
Q_LORA = 192
KV_LORA = 128
ROPE_BASE = 10000.0
AXIS_ROPE = QK_ROPE // 2
Q_BLOCK = 128
FF_HIDDEN = ((8 * D_MODEL // 3 + 255) // 256) * 256
EPS = 1e-6
P_A = 2 * GROUP_W
P_B = 3 * B_WIDTH
P_C = GROUP_W
P_D = Q_LORA + KV_LORA + QK_ROPE
P_TOTAL = P_A + P_B + P_C + P_D
SPLITS = [GROUP_W, 2 * GROUP_W, 2 * GROUP_W + B_WIDTH, 2 * GROUP_W + 2 * B_WIDTH,
          2 * GROUP_W + 3 * B_WIDTH, 2 * GROUP_W + 3 * B_WIDTH + P_C]

kernel_name = 'hybrid_diffusion_prefix_trunk'


def rmsnorm(x, g):
    xf = x.astype(jnp.float32)
    y = xf * lax.rsqrt(jnp.mean(xf * xf, axis=-1, keepdims=True) + EPS)
    return (y * g.astype(jnp.float32)).astype(x.dtype)


def modulation(cond, w_ada, b_ada):
    m = jax.nn.silu(cond) @ w_ada + b_ada
    if cond.ndim == 2:
        m = m[:, None, :]
    return jnp.split(m, 6, axis=-1)


def rope_2d_tables(n):
    rows = n // GRID_W
    row = jnp.repeat(jnp.arange(rows, dtype=jnp.float32), GRID_W)
    col = jnp.tile(jnp.arange(GRID_W, dtype=jnp.float32), rows)
    inv = ROPE_BASE ** (-jnp.arange(0, AXIS_ROPE, 2, dtype=jnp.float32) / AXIS_ROPE)
    ang = jnp.stack([row[:, None] * inv, col[:, None] * inv], axis=1)
    return jnp.cos(ang), jnp.sin(ang)


def apply_rope_2d(x, cos, sin):
    xf = x.astype(jnp.float32)
    xr = xf.reshape(*x.shape[:-1], 2, 2, AXIS_ROPE // 2)
    x1, x2 = xr[..., 0, :], xr[..., 1, :]
    extra = x.ndim - 3
    c = cos.reshape(cos.shape[0], *([1] * extra), 2, AXIS_ROPE // 2)
    s = sin.reshape(sin.shape[0], *([1] * extra), 2, AXIS_ROPE // 2)
    out = jnp.stack([x1 * c - x2 * s, x2 * c + x1 * s], axis=-2)
    return out.reshape(x.shape).astype(x.dtype)


def chunk_mlp(u, v, spat_w, spat_b):
    b, n, _ = v.shape
    vr = v.reshape(b, n // CHUNK, CHUNK, A_GROUPS, A_CH)
    mixed = jnp.einsum('gpq,bnqgc->bnpgc', spat_w, vr) + spat_b.T[None, None, :, :, None]
    return u * mixed.reshape(b, n, GROUP_W)


def short_conv(h, gate_b, gate_c, conv_w, conv_b):
    z = gate_c * h
    zp = jnp.pad(z, ((0, 0), (1, 1), (0, 0)))
    y = zp[:, :-2] * conv_w[0] + zp[:, 1:-1] * conv_w[1] + zp[:, 2:] * conv_w[2] + conv_b
    return gate_b * y


def fourier_mix(x):
    b, n, _ = x.shape
    xg = x.astype(jnp.float32).reshape(b, n, C_GROUPS, C_CH)
    y = jnp.fft.fft2(xg, axes=(1, 3), norm='ortho').real
    return y.reshape(b, n, GROUP_W).astype(x.dtype)


def mla_project(pd, g_q_lora, w_uq, g_kv_lora):
    b, n, _ = pd.shape
    cq, ckv, k_rope = jnp.split(pd, [Q_LORA, Q_LORA + KV_LORA], axis=-1)
    q = (rmsnorm(cq, g_q_lora) @ w_uq).reshape(b, n, MLA_HEADS, QK_NOPE + QK_ROPE)
    q_nope, q_rope = q[..., :QK_NOPE], q[..., QK_NOPE:]
    return q_nope, q_rope, rmsnorm(ckv, g_kv_lora), k_rope


def mla_expand(ckv, w_ukv):
    b, n, _ = ckv.shape
    kv = (ckv @ w_ukv).reshape(b, n, MLA_HEADS, QK_NOPE + V_DIM)
    return kv[..., :QK_NOPE], kv[..., QK_NOPE:]


def mla_attend(q_nope, q_rope, k_nope, k_rope, v):
    b, n = q_nope.shape[:2]
    nb = n // Q_BLOCK
    qn_b = q_nope.reshape(b, nb, Q_BLOCK, MLA_HEADS, QK_NOPE).swapaxes(0, 1)
    qr_b = q_rope.reshape(b, nb, Q_BLOCK, MLA_HEADS, QK_ROPE).swapaxes(0, 1)
    scale = (QK_NOPE + QK_ROPE) ** -0.5

    def block(args):
        qn_i, qr_i = args
        s = (jnp.einsum('bqhd,bkhd->bhqk', qn_i, k_nope, preferred_element_type=jnp.float32)
             + jnp.einsum('bqhr,bkr->bhqk', qr_i, k_rope, preferred_element_type=jnp.float32))
        p = jax.nn.softmax(s * scale, axis=-1).astype(v.dtype)
        return jnp.einsum('bhqk,bkhd->bqhd', p, v)

    o = lax.map(block, (qn_b, qr_b))
    return o.swapaxes(0, 1).reshape(b, n, MLA_HEADS * V_DIM)


def swiglu(h, w_gate_up, w_down):
    g, u = jnp.split(h @ w_gate_up, 2, axis=-1)
    return (jax.nn.silu(g) * u) @ w_down


def trunk_layer(x, cond, ctx_ckv, ctx_krope, w_ada, b_ada, g_pre_mix, g_post_mix,
                g_pre_ffn, g_post_ffn, w_in, spat_w, spat_b, conv_w, conv_b,
                g_q_lora, w_uq, g_kv_lora, w_ukv, w_out, w_gate_up, w_down):
    shift1, scale1, gate1, shift2, scale2, gate2 = modulation(cond, w_ada, b_ada)
    h = rmsnorm(x, g_pre_mix) * (1.0 + scale1) + shift1
    pa_u, pa_v, pb_h, pb_b, pb_c, pc, pd = jnp.split(h @ w_in, SPLITS, axis=-1)
    y_a = chunk_mlp(pa_u, pa_v, spat_w, spat_b)
    y_b = short_conv(pb_h, pb_b, pb_c, conv_w, conv_b)
    y_c = fourier_mix(pc)
    q_nope, q_rope, ckv, k_rope = mla_project(pd, g_q_lora, w_uq, g_kv_lora)
    if ctx_ckv is None:
        k_nope, v = mla_expand(ckv, w_ukv)
        y_d = mla_attend(q_nope, q_rope, k_nope, k_rope, v)
    else:
        cos, sin = rope_2d_tables(x.shape[1])
        q_rope = apply_rope_2d(q_rope, cos, sin)
        k_rope = apply_rope_2d(k_rope, cos, sin)
        k_nope, v = mla_expand(jnp.concatenate([ckv, ctx_ckv], axis=1), w_ukv)
        k_rope_all = jnp.concatenate([k_rope, ctx_krope], axis=1)
        y_d = mla_attend(q_nope, q_rope, k_nope, k_rope_all, v)
    mix = jnp.concatenate([y_a, y_b, y_c, y_d], axis=-1) @ w_out
    x = x + gate1 * rmsnorm(mix, g_post_mix)
    h2 = rmsnorm(x, g_pre_ffn) * (1.0 + scale2) + shift2
    x = x + gate2 * rmsnorm(swiglu(h2, w_gate_up, w_down), g_post_ffn)
    return x, ckv, k_rope


def setup_inputs(seed: int = 0) -> dict:
    key = jax.random.key(seed)
    ks = jax.random.split(key, 24)

    def nrm(k, shape, s):
        return jax.random.normal(k, shape, jnp.float32) * s

    return {
        'x_prompt': nrm(ks[0], (BATCH, SEQ, D_MODEL), 1.0),
        'x_sample': nrm(ks[1], (DEC_BATCH, DEC_SEQ, D_MODEL), 1.0),
        'cache_ckv': nrm(ks[2], (DEC_BATCH, DEPTH, PAST_LEN, KV_LORA), 1.0),
        'cache_krope': nrm(ks[3], (DEC_BATCH, DEPTH, PAST_LEN, QK_ROPE), 1.0),
        'c': nrm(ks[4], (DEC_BATCH, D_MODEL), 1.0),
        'c_ctx': nrm(ks[5], (D_MODEL,), 1.0),
        'w_ada': nrm(ks[6], (DEPTH, D_MODEL, 6 * D_MODEL), 0.5 * D_MODEL ** -0.5),
        'b_ada': nrm(ks[7], (DEPTH, 6 * D_MODEL), 0.01),
        'g_pre_mix': 1.0 + nrm(ks[8], (DEPTH, D_MODEL), 0.01),
        'g_post_mix': 1.0 + nrm(ks[9], (DEPTH, D_MODEL), 0.01),
        'g_pre_ffn': 1.0 + nrm(ks[10], (DEPTH, D_MODEL), 0.01),
        'g_post_ffn': 1.0 + nrm(ks[11], (DEPTH, D_MODEL), 0.01),
        'w_in': nrm(ks[12], (DEPTH, D_MODEL, P_TOTAL), D_MODEL ** -0.5),
        'spat_w': nrm(ks[13], (DEPTH, A_GROUPS, CHUNK, CHUNK), CHUNK ** -0.5),
        'spat_b': 1.0 + nrm(ks[14], (DEPTH, A_GROUPS, CHUNK), 0.01),
        'conv_w': nrm(ks[15], (DEPTH, CONV_WIDTH, B_WIDTH), CONV_WIDTH ** -0.5),
        'conv_b': nrm(ks[16], (DEPTH, B_WIDTH), 0.01),
        'g_q_lora': 1.0 + nrm(ks[17], (DEPTH, Q_LORA), 0.01),
        'w_uq': nrm(ks[18], (DEPTH, Q_LORA, MLA_HEADS * (QK_NOPE + QK_ROPE)), Q_LORA ** -0.5),
        'g_kv_lora': 1.0 + nrm(ks[19], (DEPTH, KV_LORA), 0.01),
        'w_ukv': nrm(ks[20], (DEPTH, KV_LORA, MLA_HEADS * (QK_NOPE + V_DIM)), KV_LORA ** -0.5),
        'w_out': nrm(ks[21], (DEPTH, MIX_WIDTH, D_MODEL), MIX_WIDTH ** -0.5),
        'w_gate_up': nrm(ks[22], (DEPTH, D_MODEL, 2 * FF_HIDDEN), D_MODEL ** -0.5),
        'w_down': nrm(ks[23], (DEPTH, FF_HIDDEN, D_MODEL), FF_HIDDEN ** -0.5),
    }


def reference(x_prompt, x_sample, cache_ckv, cache_krope, c, c_ctx, w_ada, b_ada,
              g_pre_mix, g_post_mix, g_pre_ffn, g_post_ffn, w_in, spat_w, spat_b,
              conv_w, conv_b, g_q_lora, w_uq, g_kv_lora, w_ukv, w_out, w_gate_up, w_down):
    y_prompt = x_prompt
    y_sample = x_sample
    ckv_list = []
    krope_list = []
    for l in range(DEPTH):
        lp = (w_ada[l], b_ada[l], g_pre_mix[l], g_post_mix[l], g_pre_ffn[l], g_post_ffn[l],
              w_in[l], spat_w[l], spat_b[l], conv_w[l], conv_b[l], g_q_lora[l], w_uq[l],
              g_kv_lora[l], w_ukv[l], w_out[l], w_gate_up[l], w_down[l])
        y_prompt, ckv_l, krope_l = trunk_layer(y_prompt, c_ctx, None, None, *lp)
        ckv_list.append(ckv_l)
        krope_list.append(krope_l)
        y_sample, _, _ = trunk_layer(y_sample, c, cache_ckv[:, l], cache_krope[:, l], *lp)
    state_ckv = jnp.stack(ckv_list, axis=1)
    state_krope = jnp.stack(krope_list, axis=1)
    return (y_prompt, y_sample, state_ckv, state_krope)
```

```python
import functools
import math

import jax
import jax.numpy as jnp
from jax import lax
from jax.experimental import pallas as pl
from jax.experimental.pallas import tpu as pltpu

F32 = jnp.float32
BF16 = jnp.bfloat16

D_MODEL = 1024
DEPTH = 4
GRID_W = 64
GROUP_W = 256
A_GROUPS = 4
CHUNK = 128
C_GROUPS = 4
C_CH = 64
MLA_HEADS = 4
QK_NOPE = 64
QK_ROPE = 32
V_DIM = 64
Q_LORA = 192
KV_LORA = 128
ROPE_BASE = 10000.0
AXIS_ROPE = QK_ROPE // 2
FF_HIDDEN = 2816
EPS = 1e-6

LANES = 128
HEAD_PAD = 128
Q_LORA_PAD = 256
P_IN = 512 + 768 + 256 + Q_LORA_PAD + KV_LORA + LANES
FF_CHUNK = 256
N_FF_CHUNKS = FF_HIDDEN // FF_CHUNK
TM = 512
TQ = 256
DFT_TM = 256
VMEM_LIMIT = 56 * 1024 * 1024


def _cparams(sem):
    return pltpu.CompilerParams(dimension_semantics=sem, vmem_limit_bytes=VMEM_LIMIT)


def _layer_spec(w, layer, **kw):
    tail = w.shape[1:]
    nz = (0,) * len(tail)
    return pl.BlockSpec((None,) + tail, lambda *_: (layer,) + nz, **kw)


def _rms(x, g, n):
    ms = jnp.sum(x * x, axis=-1, keepdims=True) * (1.0 / n)
    return x * lax.rsqrt(ms + EPS) * g


def _silu(x):
    return x * jax.nn.sigmoid(x)


def _dot(a, b):
    return jnp.dot(a, b, preferred_element_type=F32)


def _dot_nt(a, b):
    return lax.dot_general(a, b, (((1,), (1,)), ((), ())), preferred_element_type=F32)


def _mod_kernel(cond_ref, w_ref, b_ref, o_ref):
    s = _silu(cond_ref[...])
    o_ref[...] = _dot(s.astype(BF16), w_ref[...].astype(BF16)) + b_ref[...]


def _modulation(cond8, w_ada, b_ada):
    nt = 6 * D_MODEL // 1024
    return pl.pallas_call(
        _mod_kernel,
        grid=(DEPTH, nt),
        in_specs=[
            pl.BlockSpec((8, D_MODEL), lambda l, j: (0, 0)),
            pl.BlockSpec((None, D_MODEL, 1024), lambda l, j: (l, 0, j)),
            pl.BlockSpec((None, 1, 1024), lambda l, j: (l, 0, j)),
        ],
        out_specs=pl.BlockSpec((None, 8, 1024), lambda l, j: (l, 0, j)),
        out_shape=jax.ShapeDtypeStruct((DEPTH, 8, 6 * D_MODEL), F32),
        compiler_params=_cparams(("arbitrary", "arbitrary")),
        name="modulation",
    )(cond8, w_ada, b_ada.reshape(DEPTH, 1, 6 * D_MODEL))


def _cachekv_kernel(ckv_ref, kr_ref, wk_ref, wv_ref, k_ref, v_ref):
    c = ckv_ref[...].astype(BF16)
    kk = _dot(c, wk_ref[...])
    kr = kr_ref[...]
    for h in range(MLA_HEADS):
        k_ref[:, h * HEAD_PAD:(h + 1) * HEAD_PAD] = (kk[:, h * HEAD_PAD:(h + 1) * HEAD_PAD] + kr).astype(BF16)
    v_ref[...] = _dot(c, wv_ref[...]).astype(BF16)


def _cache_kv(cache_ckv, cache_kr_pad, wk4, wv):
    nb, _, past, _ = cache_ckv.shape
    return pl.pallas_call(
        _cachekv_kernel,
        grid=(nb, DEPTH),
        in_specs=[
            pl.BlockSpec((None, None, past, KV_LORA), lambda b, l: (b, l, 0, 0)),
            pl.BlockSpec((None, None, past, LANES), lambda b, l: (b, l, 0, 0)),
            pl.BlockSpec((None, KV_LORA, MLA_HEADS * HEAD_PAD), lambda b, l: (l, 0, 0)),
            pl.BlockSpec((None, KV_LORA, MLA_HEADS * V_DIM), lambda b, l: (l, 0, 0)),
        ],
        out_specs=[
            pl.BlockSpec((None, None, past, MLA_HEADS * HEAD_PAD), lambda b, l: (b, l, 0, 0)),
            pl.BlockSpec((None, None, past, MLA_HEADS * V_DIM), lambda b, l: (b, l, 0, 0)),
        ],
        out_shape=[
            jax.ShapeDtypeStruct((nb, DEPTH, past, MLA_HEADS * HEAD_PAD), BF16),
            jax.ShapeDtypeStruct((nb, DEPTH, past, MLA_HEADS * V_DIM), BF16),
        ],
        compiler_params=_cparams(("arbitrary", "arbitrary")),
        name="cache_kv",
    )(cache_ckv, cache_kr_pad, wk4, wv)


def _rope(x, cos, sin):
    w = x.shape[1]
    lane = lax.broadcasted_iota(jnp.int32, x.shape, 1)
    up = pltpu.roll(x, w - 8, axis=1)
    dn = pltpu.roll(x, 8, axis=1)
    sw = jnp.where((lane & 15) < 8, up, dn)
    reps = w // LANES
    if reps > 1:
        cos = jnp.concatenate([cos] * reps, axis=1)
        sin = jnp.concatenate([sin] * reps, axis=1)
    return x * cos + sw * sin


def _premix_kernel(*refs, lat, seq_len):
    if lat:
        (x_ref, mod_ref, g_ref, win_ref, bd_ref, gq_ref, wq_ref, gkv_ref, wk_ref, wv_ref, cos_ref, sin_ref,
         pa_ref, pb_ref, a_ref, b_ref, q_ref, k_ref, v_ref) = refs
    else:
        (x_ref, mod_ref, g_ref, win_ref, bd_ref, gq_ref, wq_ref, gkv_ref, wk_ref, wv_ref,
         pa_ref, pb_ref, a_ref, b_ref, q_ref, k_ref, v_ref, ckv_ref, kr_ref) = refs
    h = _rms(x_ref[...], g_ref[...], D_MODEL) * (1.0 + mod_ref[1:2, :]) + mod_ref[0:1, :]
    hb = h.astype(BF16)
    pa_ref[...] = _dot(hb, win_ref[:, 0:512]).astype(BF16)
    pb_ref[...] = _dot(hb, win_ref[:, 512:1280]).astype(BF16)
    pc = _dot(hb, win_ref[:, 1280:1536]).astype(BF16)
    ab = _dot(pc, bd_ref[...])
    a = ab[:, 0:GROUP_W].astype(BF16)
    b = ab[:, GROUP_W:2 * GROUP_W].astype(BF16)
    if lat:
        a_ref[...] = a
        b_ref[...] = b
    else:
        for s in range(TM // seq_len):
            a_ref[:, s * GROUP_W:(s + 1) * GROUP_W] = a[s * seq_len:(s + 1) * seq_len, :]
            b_ref[:, s * GROUP_W:(s + 1) * GROUP_W] = b[s * seq_len:(s + 1) * seq_len, :]
    pd = _dot(hb, win_ref[:, 1536:P_IN])
    cq = pd[:, 0:Q_LORA_PAD]
    ckv = pd[:, Q_LORA_PAD:Q_LORA_PAD + KV_LORA]
    kr = pd[:, Q_LORA_PAD + KV_LORA:]
    qn = _rms(cq, gq_ref[...], Q_LORA)
    q4 = _dot(qn.astype(BF16), wq_ref[...])
    ckv_n = _rms(ckv, gkv_ref[...], KV_LORA)
    cb = ckv_n.astype(BF16)
    kk = _dot(cb, wk_ref[...])
    v_ref[...] = _dot(cb, wv_ref[...]).astype(BF16)
    if lat:
        q4 = _rope(q4, cos_ref[...], sin_ref[...])
        kr = _rope(kr, cos_ref[...], sin_ref[...])
    else:
        ckv_ref[...] = ckv_n
        kr_ref[...] = kr[:, 0:QK_ROPE]
    q_ref[...] = (q4 * ((QK_NOPE + QK_ROPE) ** -0.5)).astype(BF16)
    for hd in range(MLA_HEADS):
        k_ref[:, hd * HEAD_PAD:(hd + 1) * HEAD_PAD] = (kk[:, hd * HEAD_PAD:(hd + 1) * HEAD_PAD] + kr).astype(BF16)


def _premix(x, mod_l, lw, layer, lat, seq_len, rope_tabs):
    t = x.shape[0]
    nt = t // TM
    tpb = max(seq_len // TM, 1)
    if lat:
        mod_map = lambda i: (1 + i // tpb, 0, 0)
        ab_shape = (seq_len, (t // seq_len) * GROUP_W)
        ab_spec = pl.BlockSpec((TM, GROUP_W), lambda i: (i % tpb, i // tpb))
    else:
        mod_map = lambda i: (0, 0, 0)
        ab_shape = (seq_len, (t // seq_len) * GROUP_W)
        ab_spec = pl.BlockSpec((seq_len, (TM // seq_len) * GROUP_W), lambda i: (0, i))
    row = lambda i: (i, 0)
    names = ["g_pre_mix", "w_in", "bd", "g_q", "wq4", "g_kv", "wk4", "wv"]
    in_specs = [pl.BlockSpec((TM, D_MODEL), row), pl.BlockSpec((None, 6, D_MODEL), mod_map)]
    in_specs += [_layer_spec(lw[n], layer) for n in names]
    args = [x, mod_l] + [lw[n] for n in names]
    out_specs = [
        pl.BlockSpec((TM, 512), row),
        pl.BlockSpec((TM, 768), row),
        ab_spec,
        ab_spec,
        pl.BlockSpec((TM, MLA_HEADS * HEAD_PAD), row),
        pl.BlockSpec((TM, MLA_HEADS * HEAD_PAD), row),
        pl.BlockSpec((TM, MLA_HEADS * V_DIM), row),
    ]
    out_shape = [
        jax.ShapeDtypeStruct((t, 512), BF16),
        jax.ShapeDtypeStruct((t, 768), BF16),
        jax.ShapeDtypeStruct(ab_shape, BF16),
        jax.ShapeDtypeStruct(ab_shape, BF16),
        jax.ShapeDtypeStruct((t, MLA_HEADS * HEAD_PAD), BF16),
        jax.ShapeDtypeStruct((t, MLA_HEADS * HEAD_PAD), BF16),
        jax.ShapeDtypeStruct((t, MLA_HEADS * V_DIM), BF16),
    ]
    if lat:
        in_specs += [pl.BlockSpec((TM, LANES), lambda i: (i % tpb, 0))] * 2
        args += list(rope_tabs)
    else:
        out_specs += [pl.BlockSpec((TM, KV_LORA), row), pl.BlockSpec((TM, QK_ROPE), row)]
        out_shape += [jax.ShapeDtypeStruct((t, KV_LORA), F32), jax.ShapeDtypeStruct((t, QK_ROPE), F32)]
    return pl.pallas_call(
        functools.partial(_premix_kernel, lat=lat, seq_len=seq_len),
        grid=(nt,),
        in_specs=in_specs,
        out_specs=out_specs,
        out_shape=out_shape,
        compiler_params=_cparams(("arbitrary",)),
        name="premix_lat" if lat else "premix_ctx",
    )(*args)


def _seqmix_kernel(pa_ref, pb_ref, prev_ref, next_ref, ws_ref, sb_ref, cw_ref, cb_ref, o_ref, *, seq_len):
    i = pl.program_id(0)
    grp = lax.broadcasted_iota(jnp.int32, (CHUNK, GROUP_W), 1) // (GROUP_W // A_GROUPS)
    for c in range(TM // CHUNK):
        rows = slice(c * CHUNK, (c + 1) * CHUNK)
        u = pa_ref[rows, 0:GROUP_W].astype(F32)
        v = pa_ref[rows, GROUP_W:2 * GROUP_W]
        m4 = _dot(ws_ref[...], v)
        mixed = m4[0:CHUNK]
        for g in range(1, A_GROUPS):
            mixed = jnp.where(grp == g, m4[g * CHUNK:(g + 1) * CHUNK], mixed)
        o_ref[rows, 0:GROUP_W] = (u * (mixed + sb_ref[...])).astype(BF16)
    hh = pb_ref[:, 0:GROUP_W].astype(F32)
    gb = pb_ref[:, GROUP_W:2 * GROUP_W].astype(F32)
    gc = pb_ref[:, 2 * GROUP_W:3 * GROUP_W].astype(F32)
    z = gc * hh
    zp = (prev_ref[:, 2 * GROUP_W:3 * GROUP_W].astype(F32) * prev_ref[:, 0:GROUP_W].astype(F32))[15:16]
    zn = (next_ref[:, 2 * GROUP_W:3 * GROUP_W].astype(F32) * next_ref[:, 0:GROUP_W].astype(F32))[0:1]
    r = lax.broadcasted_iota(jnp.int32, (TM, GROUP_W), 0)
    pos = (r + i * TM) & (seq_len - 1)
    z_prev = jnp.where(r == 0, zp, pltpu.roll(z, 1, axis=0))
    z_prev = jnp.where(pos == 0, 0.0, z_prev)
    z_next = jnp.where(r == TM - 1, zn, pltpu.roll(z, TM - 1, axis=0))
    z_next = jnp.where(pos == seq_len - 1, 0.0, z_next)
    y = z_prev * cw_ref[0:1, :] + z * cw_ref[1:2, :] + z_next * cw_ref[2:3, :] + cb_ref[...]
    o_ref[:, GROUP_W:2 * GROUP_W] = (gb * y).astype(BF16)


def _seqmix(pa, pb, lw, layer, seq_len):
    t = pa.shape[0]
    nt = t // TM
    hb = TM // 16
    last = t // 16 - 1
    row = lambda i: (i, 0)
    names = ["ws", "sb", "conv_w", "conv_b"]
    return pl.pallas_call(
        functools.partial(_seqmix_kernel, seq_len=seq_len),
        grid=(nt,),
        in_specs=[
            pl.BlockSpec((TM, 512), row),
            pl.BlockSpec((TM, 768), row),
            pl.BlockSpec((16, 768), lambda i: (jnp.maximum(i * hb - 1, 0), 0)),
            pl.BlockSpec((16, 768), lambda i: (jnp.minimum((i + 1) * hb, last), 0)),
        ] + [_layer_spec(lw[n], layer) for n in names],
        out_specs=pl.BlockSpec((TM, 2 * GROUP_W), row),
        out_shape=jax.ShapeDtypeStruct((t, 2 * GROUP_W), BF16),
        compiler_params=_cparams(("arbitrary",)),
        name="seqmix",
    )(pa, pb, pb, pb, *[lw[n] for n in names])


def _posdft_kernel(c_ref, s_ref, a_ref, b_ref, o_ref):
    o_ref[...] = (_dot(c_ref[...], a_ref[...]) + _dot(s_ref[...], b_ref[...])).astype(BF16)


def _posdft(cmat, smat_neg, a, b):
    n, cols = a.shape
    tn = min(cols, 1024)
    tmk = min(n, DFT_TM)
    return pl.pallas_call(
        _posdft_kernel,
        grid=(cols // tn, n // tmk),
        in_specs=[
            pl.BlockSpec((tmk, n), lambda j, i: (i, 0)),
            pl.BlockSpec((tmk, n), lambda j, i: (i, 0)),
            pl.BlockSpec((n, tn), lambda j, i: (0, j)),
            pl.BlockSpec((n, tn), lambda j, i: (0, j)),
        ],
        out_specs=pl.BlockSpec((tmk, tn), lambda j, i: (i, j)),
        out_shape=jax.ShapeDtypeStruct((n, cols), BF16),
        compiler_params=_cparams(("arbitrary", "arbitrary")),
        name="posdft",
    )(cmat, smat_neg, a, b)


def _attn_kernel(*refs, has_cache):
    if has_cache:
        q_ref, k_ref, v_ref, kc_ref, vc_ref, o_ref = refs
    else:
        q_ref, k_ref, v_ref, o_ref = refs
    tq = q_ref.shape[0]
    head = lax.broadcasted_iota(jnp.int32, (tq, MLA_HEADS * V_DIM), 1) // V_DIM
    out = jnp.zeros((tq, MLA_HEADS * V_DIM), F32)
    for h in range(MLA_HEADS):
        cols = slice(h * HEAD_PAD, (h + 1) * HEAD_PAD)
        qh = q_ref[:, cols]
        s = _dot_nt(qh, k_ref[:, cols])
        m = jnp.max(s, axis=-1, keepdims=True)
        if has_cache:
            sc = _dot_nt(qh, kc_ref[:, cols])
            m = jnp.maximum(m, jnp.max(sc, axis=-1, keepdims=True))
        p = jnp.exp(s - m)
        l = jnp.sum(p, axis=-1, keepdims=True)
        o = _dot(p.astype(BF16), v_ref[...])
        if has_cache:
            pc = jnp.exp(sc - m)
            l = l + jnp.sum(pc, axis=-1, keepdims=True)
            o = o + _dot(pc.astype(BF16), vc_ref[...])
        out = jnp.where(head == h, o / l, out)
    o_ref[...] = out.astype(BF16)


def _attention(q, k, v, nb, seq_len, cache=None, layer=0):
    tq = min(seq_len, TQ)
    q3 = q.reshape(nb, seq_len, MLA_HEADS * HEAD_PAD)
    k3 = k.reshape(nb, seq_len, MLA_HEADS * HEAD_PAD)
    v3 = v.reshape(nb, seq_len, MLA_HEADS * V_DIM)
    in_specs = [
        pl.BlockSpec((None, tq, MLA_HEADS * HEAD_PAD), lambda b, i: (b, i, 0)),
        pl.BlockSpec((None, seq_len, MLA_HEADS * HEAD_PAD), lambda b, i: (b, 0, 0)),
        pl.BlockSpec((None, seq_len, MLA_HEADS * V_DIM), lambda b, i: (b, 0, 0)),
    ]
    args = [q3, k3, v3]
    if cache is not None:
        kc, vc = cache
        past = kc.shape[2]
        in_specs += [
            pl.BlockSpec((None, None, past, MLA_HEADS * HEAD_PAD), lambda b, i: (b, layer, 0, 0)),
            pl.BlockSpec((None, None, past, MLA_HEADS * V_DIM), lambda b, i: (b, layer, 0, 0)),
        ]
        args += [kc, vc]
    out = pl.pallas_call(
        functools.partial(_attn_kernel, has_cache=cache is not None),
        grid=(nb, seq_len // tq),
        in_specs=in_specs,
        out_specs=pl.BlockSpec((None, tq, MLA_HEADS * V_DIM), lambda b, i: (b, i, 0)),
        out_shape=jax.ShapeDtypeStruct((nb, seq_len, MLA_HEADS * V_DIM), BF16),
        compiler_params=_cparams(("arbitrary", "arbitrary")),
        name="attn_lat" if cache is not None else "attn_ctx",
    )(*args)
    return out.reshape(nb * seq_len, MLA_HEADS * V_DIM)


def _post_kernel(x_ref, mod_ref, yab_ref, yc_ref, yd_ref, wo_ref, gpm_ref, gpf_ref, gpo_ref,
                 wg_ref, wu_ref, wd_ref, o_ref, x1_scr, h2_scr, acc_scr, *, lat, seq_len):
    if lat:
        yc = yc_ref[...]
    else:
        yc = jnp.concatenate([yc_ref[:, s * GROUP_W:(s + 1) * GROUP_W] for s in range(TM // seq_len)], axis=0)
    mix = (_dot(yab_ref[...], wo_ref[0:2 * GROUP_W, :]) + _dot(yc, wo_ref[2 * GROUP_W:3 * GROUP_W, :])
           + _dot(yd_ref[...], wo_ref[3 * GROUP_W:4 * GROUP_W, :]))
    x1 = x_ref[...] + mod_ref[2:3, :] * _rms(mix, gpm_ref[...], D_MODEL)
    x1_scr[...] = x1
    h2 = _rms(x1, gpf_ref[...], D_MODEL) * (1.0 + mod_ref[4:5, :]) + mod_ref[3:4, :]
    h2_scr[...] = h2.astype(BF16)
    acc_scr[...] = jnp.zeros_like(acc_scr)

    def body(c, carry):
        hb = h2_scr[...]
        g = _dot(hb, wg_ref[c])
        u = _dot(hb, wu_ref[c])
        act = (_silu(g) * u).astype(BF16)
        acc_scr[...] += _dot(act, wd_ref[c])
        return carry

    lax.fori_loop(0, N_FF_CHUNKS, body, 0)
    o_ref[...] = x1_scr[...] + mod_ref[5:6, :] * _rms(acc_scr[...], gpo_ref[...], D_MODEL)


def _post(x, mod_l, yab, yc, yd, lw, layer, lat, seq_len):
    t = x.shape[0]
    nt = t // TM
    tpb = max(seq_len // TM, 1)
    if lat:
        mod_map = lambda i: (1 + i // tpb, 0, 0)
        yc_spec = pl.BlockSpec((TM, GROUP_W), lambda i: (i % tpb, i // tpb))
    else:
        mod_map = lambda i: (0, 0, 0)
        yc_spec = pl.BlockSpec((seq_len, (TM // seq_len) * GROUP_W), lambda i: (0, i))
    row = lambda i: (i, 0)
    resident = dict(pipeline_mode=pl.Buffered(1))
    return pl.pallas_call(
        functools.partial(_post_kernel, lat=lat, seq_len=seq_len),
        grid=(nt,),
        in_specs=[
            pl.BlockSpec((TM, D_MODEL), row),
            pl.BlockSpec((None, 6, D_MODEL), mod_map),
            pl.BlockSpec((TM, 2 * GROUP_W), row),
            yc_spec,
            pl.BlockSpec((TM, GROUP_W), row),
            _layer_spec(lw["w_out"], layer, **resident),
            _layer_spec(lw["g_post_mix"], layer),
            _layer_spec(lw["g_pre_ffn"], layer),
            _layer_spec(lw["g_post_ffn"], layer),
            _layer_spec(lw["wg"], layer, **resident),
            _layer_spec(lw["wu"], layer, **resident),
            _layer_spec(lw["wd"], layer, **resident),
        ],
        out_specs=pl.BlockSpec((TM, D_MODEL), row),
        out_shape=jax.ShapeDtypeStruct((t, D_MODEL), F32),
        scratch_shapes=[
            pltpu.VMEM((TM, D_MODEL), F32),
            pltpu.VMEM((TM, D_MODEL), BF16),
            pltpu.VMEM((TM, D_MODEL), F32),
        ],
        compiler_params=_cparams(("arbitrary",)),
        name="post_lat" if lat else "post_ctx",
    )(x, mod_l, yab, yc, yd, lw["w_out"], lw["g_post_mix"], lw["g_pre_ffn"], lw["g_post_ffn"],
      lw["wg"], lw["wu"], lw["wd"])


def _dft_tables(n):
    scale = n ** -0.5
    if n <= 256:
        k = jnp.arange(n, dtype=jnp.int32)
        ang = ((k[:, None] * k[None, :]) % n).astype(F32) * (2.0 * math.pi / n)
        return (jnp.cos(ang) * scale).astype(BF16), (-jnp.sin(ang) * scale).astype(BF16)
    m = n // 64
    k = jnp.arange(n, dtype=jnp.int32)[:, None]
    j1 = jnp.arange(m, dtype=jnp.int32)[None, :]
    j2 = jnp.arange(64, dtype=jnp.int32)[None, :]
    ang1 = ((k * j1) % m).astype(F32) * (2.0 * math.pi / m)
    ang2 = ((k * j2) % n).astype(F32) * (2.0 * math.pi / n)
    c1, s1 = jnp.cos(ang1)[:, :, None], jnp.sin(ang1)[:, :, None]
    c2, s2 = jnp.cos(ang2)[:, None, :], jnp.sin(ang2)[:, None, :]
    cmat = ((c1 * c2 - s1 * s2) * scale).reshape(n, n)
    smat = ((s1 * c2 + c1 * s2) * scale).reshape(n, n)
    return cmat.astype(BF16), (-smat).astype(BF16)


def _channel_dft():
    c = jnp.arange(C_CH, dtype=jnp.int32)
    ang = ((c[:, None] * c[None, :]) % C_CH).astype(F32) * (2.0 * math.pi / C_CH)
    eye = jnp.eye(C_GROUPS, dtype=F32)
    bc = jnp.kron(eye, jnp.cos(ang) * C_CH ** -0.5)
    bs = jnp.kron(eye, jnp.sin(ang) * C_CH ** -0.5)
    return jnp.concatenate([bc, bs], axis=1).astype(BF16)


def _rope_tables(n):
    pos = jnp.arange(n, dtype=jnp.int32)
    row = (pos // GRID_W).astype(F32)
    col = (pos % GRID_W).astype(F32)
    inv = ROPE_BASE ** (-jnp.arange(0, AXIS_ROPE, 2, dtype=F32) / AXIS_ROPE)
    half = AXIS_ROPE // 2
    ang = jnp.concatenate([row[:, None] * inv, row[:, None] * inv, col[:, None] * inv, col[:, None] * inv], axis=1)
    sign = jnp.tile(jnp.concatenate([-jnp.ones((half,), F32), jnp.ones((half,), F32)]), 2)
    cos = jnp.concatenate([jnp.cos(ang), jnp.ones((n, LANES - QK_ROPE), F32)], axis=1)
    sin = jnp.concatenate([jnp.sin(ang) * sign, jnp.zeros((n, LANES - QK_ROPE), F32)], axis=1)
    return cos, sin


def _layout_weights(g_pre_mix, g_post_mix, g_pre_ffn, g_post_ffn, w_in, spat_w, spat_b, conv_w, conv_b,
                    g_q_lora, w_uq, g_kv_lora, w_ukv, w_out, w_gate_up, w_down):
    d = DEPTH
    zeros = lambda *s: jnp.zeros(s, F32)
    c_q0 = 1536
    c_kv0 = c_q0 + Q_LORA
    c_kr0 = c_kv0 + KV_LORA
    w_in_p = jnp.concatenate([
        w_in[:, :, :c_kv0], zeros(d, D_MODEL, Q_LORA_PAD - Q_LORA),
        w_in[:, :, c_kv0:c_kr0],
        w_in[:, :, c_kr0:], zeros(d, D_MODEL, LANES - QK_ROPE)], axis=2).astype(BF16)
    uq = w_uq.reshape(d, Q_LORA, MLA_HEADS, QK_NOPE + QK_ROPE)
    wq4 = jnp.concatenate([uq[..., QK_NOPE:], uq[..., :QK_NOPE],
                           zeros(d, Q_LORA, MLA_HEADS, HEAD_PAD - QK_NOPE - QK_ROPE)], axis=3)
    wq4 = wq4.reshape(d, Q_LORA, MLA_HEADS * HEAD_PAD)
    wq4 = jnp.concatenate([wq4, zeros(d, Q_LORA_PAD - Q_LORA, MLA_HEADS * HEAD_PAD)], axis=1).astype(BF16)
    ukv = w_ukv.reshape(d, KV_LORA, MLA_HEADS, QK_NOPE + V_DIM)
    wk4 = jnp.concatenate([zeros(d, KV_LORA, MLA_HEADS, QK_ROPE), ukv[..., :QK_NOPE],
                           zeros(d, KV_LORA, MLA_HEADS, HEAD_PAD - QK_NOPE - QK_ROPE)], axis=3)
    wk4 = wk4.reshape(d, KV_LORA, MLA_HEADS * HEAD_PAD).astype(BF16)
    wv = ukv[..., QK_NOPE:].reshape(d, KV_LORA, MLA_HEADS * V_DIM).astype(BF16)
    g_q = jnp.concatenate([g_q_lora, zeros(d, Q_LORA_PAD - Q_LORA)], axis=1)
    wgu = w_gate_up.astype(BF16)
    wg = wgu[:, :, :FF_HIDDEN].reshape(d, D_MODEL, N_FF_CHUNKS, FF_CHUNK).transpose(0, 2, 1, 3)
    wu = wgu[:, :, FF_HIDDEN:].reshape(d, D_MODEL, N_FF_CHUNKS, FF_CHUNK).transpose(0, 2, 1, 3)
    wd = w_down.astype(BF16).reshape(d, N_FF_CHUNKS, FF_CHUNK, D_MODEL)
    stacked = dict(
        g_pre_mix=g_pre_mix.reshape(d, 1, D_MODEL), g_post_mix=g_post_mix.reshape(d, 1, D_MODEL),
        g_pre_ffn=g_pre_ffn.reshape(d, 1, D_MODEL), g_post_ffn=g_post_ffn.reshape(d, 1, D_MODEL),
        w_in=w_in_p, wq4=wq4, wk4=wk4, wv=wv,
        g_q=g_q.reshape(d, 1, Q_LORA_PAD), g_kv=g_kv_lora.reshape(d, 1, KV_LORA),
        ws=spat_w.reshape(d, A_GROUPS * CHUNK, CHUNK).astype(BF16),
        sb=jnp.repeat(spat_b.transpose(0, 2, 1), GROUP_W // A_GROUPS, axis=2),
        conv_w=conv_w, conv_b=conv_b.reshape(d, 1, GROUP_W),
        w_out=w_out.astype(BF16), wg=wg, wu=wu, wd=wd,
    )
    return stacked


def _trunk_layer(x, mod_l, lw, layer, lat, nb, seq_len, dft, rope_tabs, cache):
    outs = _premix(x, mod_l, lw, layer, lat, seq_len, rope_tabs)
    pa, pb, a, b, q, k, v = outs[:7]
    yab = _seqmix(pa, pb, lw, layer, seq_len)
    yc = _posdft(dft[0], dft[1], a, b)
    yd = _attention(q, k, v, nb, seq_len, cache, layer)
    x = _post(x, mod_l, yab, yc, yd, lw, layer, lat, seq_len)
    return x, outs[7:]


def kernel(x_prompt, x_sample, cache_ckv, cache_krope, c, c_ctx, w_ada, b_ada, g_pre_mix, g_post_mix, g_pre_ffn,
           g_post_ffn, w_in, spat_w, spat_b, conv_w, conv_b, g_q_lora, w_uq, g_kv_lora, w_ukv, w_out, w_gate_up,
           w_down):
    batch, seq, _ = x_prompt.shape
    dec_batch, dec_seq, _ = x_sample.shape
    assert (batch * seq) % TM == 0 and TM % seq == 0 and dec_seq % TM == 0 and dec_batch + 1 <= 8

    stacked = _layout_weights(g_pre_mix, g_post_mix, g_pre_ffn, g_post_ffn, w_in, spat_w, spat_b, conv_w, conv_b,
                              g_q_lora, w_uq, g_kv_lora, w_ukv, w_out, w_gate_up, w_down)
    stacked["bd"] = jnp.broadcast_to(_channel_dft(), (DEPTH, GROUP_W, 2 * GROUP_W))
    dft_ctx = _dft_tables(seq)
    dft_lat = _dft_tables(dec_seq)
    rope_tabs = _rope_tables(dec_seq)

    cond8 = jnp.concatenate([c_ctx[None, :], c, jnp.zeros((8 - 1 - dec_batch, D_MODEL), F32)], axis=0)
    mod = _modulation(cond8, w_ada, b_ada).reshape(DEPTH, 8, 6, D_MODEL)

    cache_kr_pad = jnp.pad(cache_krope, ((0, 0), (0, 0), (0, 0), (0, LANES - QK_ROPE)))
    cache = _cache_kv(cache_ckv, cache_kr_pad, stacked["wk4"], stacked["wv"])

    y_ctx = x_prompt.reshape(batch * seq, D_MODEL)
    y_lat = x_sample.reshape(dec_batch * dec_seq, D_MODEL)
    ckv_list, krope_list = [], []
    for l in range(DEPTH):
        y_ctx, (ckv_l, kr_l) = _trunk_layer(y_ctx, mod[l], stacked, l, False, batch, seq, dft_ctx, None, None)
        ckv_list.append(ckv_l.reshape(batch, seq, KV_LORA))
        krope_list.append(kr_l.reshape(batch, seq, QK_ROPE))
        y_lat, _ = _trunk_layer(y_lat, mod[l], stacked, l, True, dec_batch, dec_seq, dft_lat, rope_tabs, cache)
    state_ckv = jnp.stack(ckv_list, axis=1)
    state_krope = jnp.stack(krope_list, axis=1)
    return (y_ctx.reshape(batch, seq, D_MODEL), y_lat.reshape(dec_batch, dec_seq, D_MODEL), state_ckv, state_krope)
```

```python
import functools
import math

import jax
import jax.numpy as jnp
from jax import lax
from jax.experimental import pallas as pl
from jax.experimental.pallas import tpu as pltpu

F32 = jnp.float32
BF16 = jnp.bfloat16

D_MODEL = 1024
DEPTH = 4
GRID_W = 64
GROUP_W = 256
A_GROUPS = 4
CHUNK = 128
C_GROUPS = 4
C_CH = 64
MLA_HEADS = 4
QK_NOPE = 64
QK_ROPE = 32
V_DIM = 64
Q_LORA = 192
KV_LORA = 128
ROPE_BASE = 10000.0
AXIS_ROPE = QK_ROPE // 2
FF_HIDDEN = 2816
EPS = 1e-6

LANES = 128
HEAD_PAD = 128
Q_LORA_PAD = 256
P_MAIN = 512 + 768 + 256
QW = MLA_HEADS * HEAD_PAD
VW = MLA_HEADS * V_DIM
FF_CHUNK = 256
N_FF_CHUNKS = FF_HIDDEN // FF_CHUNK
TM = 512
TQ = 512
ATT_SUB = 256
ATT_KC = 512
ATT_EB = 128
DFT_TM = 256
VMEM_LIMIT = 56 * 1024 * 1024
Q_SCALE = (QK_NOPE + QK_ROPE) ** -0.5 * math.log2(math.e)


def _cparams(sem, **flags):
    return pltpu.CompilerParams(dimension_semantics=sem, vmem_limit_bytes=VMEM_LIMIT, flags=flags or None)


def _layer_spec(w, layer, **kw):
    tail = w.shape[1:]
    nz = (0,) * len(tail)
    return pl.BlockSpec((None,) + tail, lambda *_: (layer,) + nz, **kw)


def _rms_scale(x, n):
    ms = jnp.sum(x * x, axis=-1, keepdims=True) * (1.0 / n)
    return x * lax.rsqrt(ms + EPS)


def _silu(x):
    return x * jax.nn.sigmoid(x)


def _dot(a, b):
    return jnp.dot(a, b, preferred_element_type=F32)


def _dot_nt(a, b):
    return lax.dot_general(a, b, (((1,), (1,)), ((), ())), preferred_element_type=F32)


def _mod_kernel(cond_ref, w_ref, b_ref, o_ref):
    s = _silu(cond_ref[...])
    o_ref[...] = _dot(s.astype(BF16), w_ref[...].astype(BF16)) + b_ref[...]


def _modulation(cond8, w_ada, b_ada):
    nt = 6 * D_MODEL // 1024
    return pl.pallas_call(
        _mod_kernel,
        grid=(DEPTH, nt),
        in_specs=[
            pl.BlockSpec((8, D_MODEL), lambda l, j: (0, 0)),
            pl.BlockSpec((None, D_MODEL, 1024), lambda l, j: (l, 0, j)),
            pl.BlockSpec((None, 1, 1024), lambda l, j: (l, 0, j)),
        ],
        out_specs=pl.BlockSpec((None, 8, 1024), lambda l, j: (l, 0, j)),
        out_shape=jax.ShapeDtypeStruct((DEPTH, 8, 6 * D_MODEL), F32),
        compiler_params=_cparams(("arbitrary", "arbitrary")),
        name="modulation",
    )(cond8, w_ada, b_ada.reshape(DEPTH, 1, 6 * D_MODEL))


def _cachekv_kernel(ckv_ref, kr_ref, wk_ref, wvt_ref, k_ref, vt_ref):
    c = ckv_ref[...].astype(BF16)
    kk = _dot(c, wk_ref[...])
    kr = kr_ref[...]
    for h in range(MLA_HEADS):
        k_ref[:, h * HEAD_PAD:(h + 1) * HEAD_PAD] = (kk[:, h * HEAD_PAD:(h + 1) * HEAD_PAD] + kr).astype(BF16)
    vt_ref[...] = _dot_nt(wvt_ref[...], c).astype(BF16)


def _cache_kv(cache_ckv, cache_kr_pad, wk4, wvt):
    nb, _, past, _ = cache_ckv.shape
    return pl.pallas_call(
        _cachekv_kernel,
        grid=(nb, DEPTH),
        in_specs=[
            pl.BlockSpec((None, None, past, KV_LORA), lambda b, l: (b, l, 0, 0)),
            pl.BlockSpec((None, None, past, LANES), lambda b, l: (b, l, 0, 0)),
            pl.BlockSpec((None, KV_LORA, QW), lambda b, l: (l, 0, 0)),
            pl.BlockSpec((None, VW, KV_LORA), lambda b, l: (l, 0, 0)),
        ],
        out_specs=[
            pl.BlockSpec((None, None, past, QW), lambda b, l: (b, l, 0, 0)),
            pl.BlockSpec((None, None, VW, past), lambda b, l: (b, l, 0, 0)),
        ],
        out_shape=[
            jax.ShapeDtypeStruct((nb, DEPTH, past, QW), BF16),
            jax.ShapeDtypeStruct((nb, DEPTH, VW, past), BF16),
        ],
        compiler_params=_cparams(("arbitrary", "arbitrary")),
        name="cache_kv",
    )(cache_ckv, cache_kr_pad, wk4, wvt)


def _rope(x, cos, sin):
    w = x.shape[1]
    lane = lax.broadcasted_iota(jnp.int32, x.shape, 1)
    up = pltpu.roll(x, w - 8, axis=1)
    dn = pltpu.roll(x, 8, axis=1)
    sw = jnp.where((lane & 15) < 8, up, dn)
    reps = w // LANES
    if reps > 1:
        cos = jnp.concatenate([cos] * reps, axis=1)
        sin = jnp.concatenate([sin] * reps, axis=1)
    return x * cos + sw * sin


def _premix_kernel(*refs, lat, seq_len):
    if lat:
        (x_ref, mod_ref, g_ref, win_ref, wcq_ref, wckv_ref, wkr_ref, bd_ref, gq_ref, wq_ref, gkv_ref, wk_ref,
         wvt_ref, cos_ref, sin_ref, pa_ref, pb_ref, a_ref, b_ref, q_ref, k_ref, vt_ref) = refs
    else:
        (x_ref, mod_ref, g_ref, win_ref, wcq_ref, wckv_ref, wkr_ref, bd_ref, gq_ref, wq_ref, gkv_ref, wk_ref,
         wvt_ref, pa_ref, pb_ref, a_ref, b_ref, q_ref, k_ref, vt_ref, ckv_ref, kr_ref) = refs
    gain = g_ref[...] * (1.0 + mod_ref[1:2, :])
    h = _rms_scale(x_ref[...], D_MODEL) * gain + mod_ref[0:1, :]
    hb = h.astype(BF16)
    pa_ref[...] = _dot(hb, win_ref[:, 0:512]).astype(BF16)
    pb_ref[...] = _dot(hb, win_ref[:, 512:1280]).astype(BF16)
    pc = _dot(hb, win_ref[:, 1280:P_MAIN]).astype(BF16)
    ab = _dot(pc, bd_ref[...])
    a = ab[:, 0:GROUP_W].astype(BF16)
    b = ab[:, GROUP_W:2 * GROUP_W].astype(BF16)
    if lat:
        a_ref[...] = a
        b_ref[...] = b
    else:
        for s in range(TM // seq_len):
            a_ref[:, s * GROUP_W:(s + 1) * GROUP_W] = a[s * seq_len:(s + 1) * seq_len, :]
            b_ref[:, s * GROUP_W:(s + 1) * GROUP_W] = b[s * seq_len:(s + 1) * seq_len, :]
    cq = _dot(hb, wcq_ref[...])
    ckv = _dot(hb, wckv_ref[...])
    kr = _dot(hb, wkr_ref[...])
    qn = _rms_scale(cq, Q_LORA) * gq_ref[...]
    q4 = _dot(qn.astype(BF16), wq_ref[...])
    ckv_n = _rms_scale(ckv, KV_LORA) * gkv_ref[...]
    cb = ckv_n.astype(BF16)
    kk = _dot(cb, wk_ref[...])
    vt = _dot_nt(wvt_ref[...], cb).astype(BF16)
    if lat:
        vt_ref[...] = vt
        q4 = _rope(q4, cos_ref[...], sin_ref[...])
        kr = _rope(kr, cos_ref[...], sin_ref[...])
    else:
        for s in range(TM // seq_len):
            vt_ref[s] = vt[:, s * seq_len:(s + 1) * seq_len]
        ckv_ref[...] = ckv_n
        kr_ref[...] = kr[:, 0:QK_ROPE]
    q_ref[...] = (q4 * Q_SCALE).astype(BF16)
    for hd in range(MLA_HEADS):
        k_ref[:, hd * HEAD_PAD:(hd + 1) * HEAD_PAD] = (kk[:, hd * HEAD_PAD:(hd + 1) * HEAD_PAD] + kr).astype(BF16)


def _premix(x, mod_l, lw, layer, lat, seq_len, rope_tabs):
    t = x.shape[0]
    nt = t // TM
    nb = t // seq_len
    tpb = max(seq_len // TM, 1)
    ab_shape = (seq_len, nb * GROUP_W)
    if lat:
        mod_map = lambda i: (1 + i // tpb, 0, 0)
        ab_spec = pl.BlockSpec((TM, GROUP_W), lambda i: (i % tpb, i // tpb))
        vt_spec = pl.BlockSpec((None, VW, TM), lambda i: (i // tpb, 0, i % tpb))
    else:
        mod_map = lambda i: (0, 0, 0)
        ab_spec = pl.BlockSpec((seq_len, (TM // seq_len) * GROUP_W), lambda i: (0, i))
        vt_spec = pl.BlockSpec((TM // seq_len, VW, seq_len), lambda i: (i, 0, 0))
    row = lambda i: (i, 0)
    names = ["g_pre_mix", "w_main", "w_cq", "w_ckv", "w_kr", "bd", "g_q", "wq4", "g_kv", "wk4", "wvt"]
    in_specs = [pl.BlockSpec((TM, D_MODEL), row), pl.BlockSpec((None, 6, D_MODEL), mod_map)]
    in_specs += [_layer_spec(lw[n], layer) for n in names]
    args = [x, mod_l] + [lw[n] for n in names]
    out_specs = [
        pl.BlockSpec((TM, 512), row),
        pl.BlockSpec((TM, 768), row),
        ab_spec,
        ab_spec,
        pl.BlockSpec((TM, QW), row),
        pl.BlockSpec((TM, QW), row),
        vt_spec,
    ]
    out_shape = [
        jax.ShapeDtypeStruct((t, 512), BF16),
        jax.ShapeDtypeStruct((t, 768), BF16),
        jax.ShapeDtypeStruct(ab_shape, BF16),
        jax.ShapeDtypeStruct(ab_shape, BF16),
        jax.ShapeDtypeStruct((t, QW), BF16),
        jax.ShapeDtypeStruct((t, QW), BF16),
        jax.ShapeDtypeStruct((nb, VW, seq_len), BF16),
    ]
    if lat:
        in_specs += [pl.BlockSpec((TM, LANES), lambda i: (i % tpb, 0))] * 2
        args += list(rope_tabs)
    else:
        out_specs += [pl.BlockSpec((TM, KV_LORA), row), pl.BlockSpec((TM, QK_ROPE), row)]
        out_shape += [jax.ShapeDtypeStruct((t, KV_LORA), F32), jax.ShapeDtypeStruct((t, QK_ROPE), F32)]
    return pl.pallas_call(
        functools.partial(_premix_kernel, lat=lat, seq_len=seq_len),
        grid=(nt,),
        in_specs=in_specs,
        out_specs=out_specs,
        out_shape=out_shape,
        compiler_params=_cparams(("arbitrary",)),
        name="premix_lat" if lat else "premix_ctx",
    )(*args)


def _seqmix_kernel(pa_ref, pb_ref, prev_ref, next_ref, ws_ref, sb_ref, cw_ref, cb_ref, o_ref, *, seq_len):
    i = pl.program_id(0)
    grp = lax.broadcasted_iota(jnp.int32, (CHUNK, GROUP_W), 1) // (GROUP_W // A_GROUPS)
    for c in range(TM // CHUNK):
        rows = slice(c * CHUNK, (c + 1) * CHUNK)
        u = pa_ref[rows, 0:GROUP_W].astype(F32)
        v = pa_ref[rows, GROUP_W:2 * GROUP_W]
        m4 = _dot(ws_ref[...], v)
        mixed = m4[0:CHUNK]
        for g in range(1, A_GROUPS):
            mixed = jnp.where(grp == g, m4[g * CHUNK:(g + 1) * CHUNK], mixed)
        o_ref[rows, 0:GROUP_W] = (u * (mixed + sb_ref[...])).astype(BF16)
    hh = pb_ref[:, 0:GROUP_W].astype(F32)
    gb = pb_ref[:, GROUP_W:2 * GROUP_W].astype(F32)
    gc = pb_ref[:, 2 * GROUP_W:3 * GROUP_W].astype(F32)
    z = gc * hh
    zp = (prev_ref[:, 2 * GROUP_W:3 * GROUP_W].astype(F32) * prev_ref[:, 0:GROUP_W].astype(F32))[15:16]
    zn = (next_ref[:, 2 * GROUP_W:3 * GROUP_W].astype(F32) * next_ref[:, 0:GROUP_W].astype(F32))[0:1]
    r = lax.broadcasted_iota(jnp.int32, (TM, GROUP_W), 0)
    pos = (r + i * TM) & (seq_len - 1)
    z_prev = jnp.where(r == 0, zp, pltpu.roll(z, 1, axis=0))
    z_prev = jnp.where(pos == 0, 0.0, z_prev)
    z_next = jnp.where(r == TM - 1, zn, pltpu.roll(z, TM - 1, axis=0))
    z_next = jnp.where(pos == seq_len - 1, 0.0, z_next)
    y = z_prev * cw_ref[0:1, :] + z * cw_ref[1:2, :] + z_next * cw_ref[2:3, :] + cb_ref[...]
    o_ref[:, GROUP_W:2 * GROUP_W] = (gb * y).astype(BF16)


def _seqmix(pa, pb, lw, layer, seq_len):
    t = pa.shape[0]
    nt = t // TM
    hb = TM // 16
    last = t // 16 - 1
    row = lambda i: (i, 0)
    names = ["ws", "sb", "conv_w", "conv_b"]
    return pl.pallas_call(
        functools.partial(_seqmix_kernel, seq_len=seq_len),
        grid=(nt,),
        in_specs=[
            pl.BlockSpec((TM, 512), row),
            pl.BlockSpec((TM, 768), row),
            pl.BlockSpec((16, 768), lambda i: (jnp.maximum(i * hb - 1, 0), 0)),
            pl.BlockSpec((16, 768), lambda i: (jnp.minimum((i + 1) * hb, last), 0)),
        ] + [_layer_spec(lw[n], layer) for n in names],
        out_specs=pl.BlockSpec((TM, 2 * GROUP_W), row),
        out_shape=jax.ShapeDtypeStruct((t, 2 * GROUP_W), BF16),
        compiler_params=_cparams(("arbitrary",)),
        name="seqmix",
    )(pa, pb, pb, pb, *[lw[n] for n in names])


def _posdft_kernel(c_ref, s_ref, a_ref, b_ref, o_ref):
    o_ref[...] = (_dot(c_ref[...], a_ref[...]) + _dot(s_ref[...], b_ref[...])).astype(BF16)


def _posdft(cmat, smat_neg, a, b):
    n, cols = a.shape
    tn = min(cols, 1024)
    tmk = min(n, DFT_TM)
    return pl.pallas_call(
        _posdft_kernel,
        grid=(cols // tn, n // tmk),
        in_specs=[
            pl.BlockSpec((tmk, n), lambda j, i: (i, 0)),
            pl.BlockSpec((tmk, n), lambda j, i: (i, 0)),
            pl.BlockSpec((n, tn), lambda j, i: (0, j)),
            pl.BlockSpec((n, tn), lambda j, i: (0, j)),
        ],
        out_specs=pl.BlockSpec((tmk, tn), lambda j, i: (i, j)),
        out_shape=jax.ShapeDtypeStruct((n, cols), BF16),
        compiler_params=_cparams(("arbitrary", "arbitrary")),
        name="posdft",
    )(cmat, smat_neg, a, b)


def _attn_kernel(*refs, has_cache):
    if has_cache:
        q_ref, k_ref, vt_ref, kc_ref, vct_ref, o_ref, st_scr, pt_scr = refs
    else:
        q_ref, k_ref, vt_ref, o_ref, st_scr, pt_scr = refs
    nk = k_ref.shape[0]
    kc = min(nk, ATT_KC)
    segs = [(k_ref, vt_ref, c * kc, c * kc, kc) for c in range(nk // kc)]
    if has_cache:
        segs.append((kc_ref, vct_ref, 0, nk, kc_ref.shape[0]))
    nsub = q_ref.shape[0] // ATT_SUB
    items = [(s, h) for s in range(nsub) for h in range(MLA_HEADS)]
    outs = [[None] * MLA_HEADS for _ in range(nsub)]
    m_prev = None
    for t in range(len(items) + 1):
        if t < len(items):
            s, h = items[t]
            cols = slice(h * HEAD_PAD, (h + 1) * HEAD_PAD)
            qh = q_ref[s * ATT_SUB:(s + 1) * ATT_SUB, cols]
            m = None
        if t >= 1:
            ps, ph = items[t - 1]
            vrows = slice(ph * V_DIM, (ph + 1) * V_DIM)
            l = jnp.zeros((1, ATT_SUB), F32)
            ot = jnp.zeros((V_DIM, ATT_SUB), F32)
        for ci, (kref, vref, r0, s0, n) in enumerate(segs):
            if t < len(items):
                st_scr[t % 2, s0:s0 + n, :] = _dot_nt(kref[r0:r0 + n, cols], qh)
                for j in range(0, n, ATT_EB):
                    cm = jnp.max(st_scr[t % 2, s0 + j:s0 + j + ATT_EB, :], axis=0, keepdims=True)
                    m = cm if m is None else jnp.maximum(m, cm)
            if t >= 1:
                for j in range(0, n, ATT_EB):
                    p = jnp.exp2(st_scr[(t - 1) % 2, s0 + j:s0 + j + ATT_EB, :] - m_prev)
                    l = l + jnp.sum(p, axis=0, keepdims=True)
                    pt_scr[ci % 2, j:j + ATT_EB, :] = p.astype(BF16)
                ot = ot + _dot(vref[vrows, r0:r0 + n], pt_scr[ci % 2, 0:n, :])
        if t >= 1:
            outs[ps][ph] = ot / l
        if t < len(items):
            m_prev = m
    for s in range(nsub):
        o_ref[s * ATT_SUB:(s + 1) * ATT_SUB, :] = jnp.concatenate(outs[s], axis=0).T.astype(BF16)


def _attention(q, k, vt, nb, seq_len, cache=None, layer=0):
    tq = min(seq_len, TQ)
    nk_total = seq_len + (cache[0].shape[2] if cache is not None else 0)
    q3 = q.reshape(nb, seq_len, QW)
    k3 = k.reshape(nb, seq_len, QW)
    in_specs = [
        pl.BlockSpec((None, tq, QW), lambda b, i: (b, i, 0)),
        pl.BlockSpec((None, seq_len, QW), lambda b, i: (b, 0, 0)),
        pl.BlockSpec((None, VW, seq_len), lambda b, i: (b, 0, 0)),
    ]
    args = [q3, k3, vt]
    if cache is not None:
        kc, vct = cache
        past = kc.shape[2]
        in_specs += [
            pl.BlockSpec((None, None, past, QW), lambda b, i: (b, layer, 0, 0)),
            pl.BlockSpec((None, None, VW, past), lambda b, i: (b, layer, 0, 0)),
        ]
        args += [kc, vct]
    out = pl.pallas_call(
        functools.partial(_attn_kernel, has_cache=cache is not None),
        grid=(nb, seq_len // tq),
        in_specs=in_specs,
        out_specs=pl.BlockSpec((None, tq, VW), lambda b, i: (b, i, 0)),
        out_shape=jax.ShapeDtypeStruct((nb, seq_len, VW), BF16),
        scratch_shapes=[pltpu.VMEM((2, nk_total, ATT_SUB), F32),
                        pltpu.VMEM((2, min(seq_len, ATT_KC), ATT_SUB), BF16)],
        compiler_params=_cparams(("arbitrary", "arbitrary")),
        name="attn_lat" if cache is not None else "attn_ctx",
    )(*args)
    return out.reshape(nb * seq_len, VW)


def _post_kernel(x_ref, mod_ref, yab_ref, yc_ref, yd_ref, wo_ref, gpm_ref, gpf_ref, gpo_ref,
                 wgu_ref, wd_ref, o_ref, x1_scr, acc_scr, *, lat, seq_len):
    if lat:
        yc = yc_ref[...]
    else:
        yc = jnp.concatenate([yc_ref[:, s * GROUP_W:(s + 1) * GROUP_W] for s in range(TM // seq_len)], axis=0)
    mix = (_dot(yab_ref[...], wo_ref[0:2 * GROUP_W, :]) + _dot(yc, wo_ref[2 * GROUP_W:3 * GROUP_W, :])
           + _dot(yd_ref[...], wo_ref[3 * GROUP_W:4 * GROUP_W, :]))
    x1 = x_ref[...] + _rms_scale(mix, D_MODEL) * (mod_ref[2:3, :] * gpm_ref[...])
    x1_scr[...] = x1
    h2 = _rms_scale(x1, D_MODEL) * (gpf_ref[...] * (1.0 + mod_ref[4:5, :])) + mod_ref[3:4, :]
    hb = h2.astype(BF16)
    for c in range(N_FF_CHUNKS):
        g = _dot(hb, wgu_ref[:, c * FF_CHUNK:(c + 1) * FF_CHUNK])
        u = _dot(hb, wgu_ref[:, FF_HIDDEN + c * FF_CHUNK:FF_HIDDEN + (c + 1) * FF_CHUNK])
        act = (_silu(g) * u).astype(BF16)
        d = _dot(act, wd_ref[c * FF_CHUNK:(c + 1) * FF_CHUNK, :])
        if c == 0:
            acc_scr[...] = d
        else:
            acc_scr[...] += d
    o_ref[...] = x1_scr[...] + _rms_scale(acc_scr[...], D_MODEL) * (mod_ref[5:6, :] * gpo_ref[...])


def _post(x, mod_l, yab, yc, yd, lw, layer, lat, seq_len):
    t = x.shape[0]
    nt = t // TM
    tpb = max(seq_len // TM, 1)
    if lat:
        mod_map = lambda i: (1 + i // tpb, 0, 0)
        yc_spec = pl.BlockSpec((TM, GROUP_W), lambda i: (i % tpb, i // tpb))
    else:
        mod_map = lambda i: (0, 0, 0)
        yc_spec = pl.BlockSpec((seq_len, (TM // seq_len) * GROUP_W), lambda i: (0, i))
    row = lambda i: (i, 0)
    resident = dict(pipeline_mode=pl.Buffered(1))
    return pl.pallas_call(
        functools.partial(_post_kernel, lat=lat, seq_len=seq_len),
        grid=(nt,),
        in_specs=[
            pl.BlockSpec((TM, D_MODEL), row),
            pl.BlockSpec((None, 6, D_MODEL), mod_map),
            pl.BlockSpec((TM, 2 * GROUP_W), row),
            yc_spec,
            pl.BlockSpec((TM, GROUP_W), row),
            _layer_spec(lw["w_out"], layer, **resident),
            _layer_spec(lw["g_post_mix"], layer),
            _layer_spec(lw["g_pre_ffn"], layer),
            _layer_spec(lw["g_post_ffn"], layer),
            _layer_spec(lw["w_gate_up"], layer, **resident),
            _layer_spec(lw["w_down"], layer, **resident),
        ],
        out_specs=pl.BlockSpec((TM, D_MODEL), row),
        out_shape=jax.ShapeDtypeStruct((t, D_MODEL), F32),
        scratch_shapes=[
            pltpu.VMEM((TM, D_MODEL), F32),
            pltpu.VMEM((TM, D_MODEL), F32),
        ],
        compiler_params=_cparams(("arbitrary",)),
        name="post_lat" if lat else "post_ctx",
    )(x, mod_l, yab, yc, yd, lw["w_out"], lw["g_post_mix"], lw["g_pre_ffn"], lw["g_post_ffn"],
      lw["w_gate_up"], lw["w_down"])


def _dft_tables(n):
    scale = n ** -0.5
    if n <= 256:
        k = jnp.arange(n, dtype=jnp.int32)
        ang = ((k[:, None] * k[None, :]) % n).astype(F32) * (2.0 * math.pi / n)
        return (jnp.cos(ang) * scale).astype(BF16), (-jnp.sin(ang) * scale).astype(BF16)
    m = n // 64
    j = jnp.arange(n, dtype=jnp.int32)[None, :]
    k1 = jnp.arange(m, dtype=jnp.int32)[:, None]
    k2 = jnp.arange(64, dtype=jnp.int32)[:, None]
    ang1 = ((k1 * j) % m).astype(F32) * (2.0 * math.pi / m)
    ang2 = ((k2 * j) % n).astype(F32) * (2.0 * math.pi / n)
    c1, s1 = jnp.cos(ang1)[:, None, :], jnp.sin(ang1)[:, None, :]
    c2, s2 = (jnp.cos(ang2) * scale)[None, :, :], (jnp.sin(ang2) * scale)[None, :, :]
    cmat = (c1 * c2 - s1 * s2).astype(BF16).reshape(n, n)
    smat_neg = (-(s1 * c2 + c1 * s2)).astype(BF16).reshape(n, n)
    return cmat, smat_neg


def _channel_dft():
    c = jnp.arange(C_CH, dtype=jnp.int32)
    ang = ((c[:, None] * c[None, :]) % C_CH).astype(F32) * (2.0 * math.pi / C_CH)
    eye = jnp.eye(C_GROUPS, dtype=F32)
    bc = jnp.kron(eye, jnp.cos(ang) * C_CH ** -0.5)
    bs = jnp.kron(eye, jnp.sin(ang) * C_CH ** -0.5)
    return jnp.concatenate([bc, bs], axis=1).astype(BF16)


def _rope_tables(n):
    pos = jnp.arange(n, dtype=jnp.int32)
    row = (pos // GRID_W).astype(F32)
    col = (pos % GRID_W).astype(F32)
    inv = ROPE_BASE ** (-jnp.arange(0, AXIS_ROPE, 2, dtype=F32) / AXIS_ROPE)
    half = AXIS_ROPE // 2
    ang = jnp.concatenate([row[:, None] * inv, row[:, None] * inv, col[:, None] * inv, col[:, None] * inv], axis=1)
    sign = jnp.tile(jnp.concatenate([-jnp.ones((half,), F32), jnp.ones((half,), F32)]), 2)
    cos = jnp.concatenate([jnp.cos(ang), jnp.ones((n, LANES - QK_ROPE), F32)], axis=1)
    sin = jnp.concatenate([jnp.sin(ang) * sign, jnp.zeros((n, LANES - QK_ROPE), F32)], axis=1)
    return cos, sin


def _layout_weights(g_pre_mix, g_post_mix, g_pre_ffn, g_post_ffn, w_in, spat_w, spat_b, conv_w, conv_b,
                    g_q_lora, w_uq, g_kv_lora, w_ukv, w_out, w_gate_up, w_down):
    d = DEPTH
    zeros = lambda *s: jnp.zeros(s, F32)
    c_kv0 = P_MAIN + Q_LORA
    c_kr0 = c_kv0 + KV_LORA
    lane_pad = lambda w, width: jnp.pad(w, ((0, 0), (0, 0), (0, width - w.shape[2]))).astype(BF16)
    uq = w_uq.reshape(d, Q_LORA, MLA_HEADS, QK_NOPE + QK_ROPE)
    wq4 = jnp.concatenate([uq[..., QK_NOPE:], uq[..., :QK_NOPE],
                           zeros(d, Q_LORA, MLA_HEADS, HEAD_PAD - QK_NOPE - QK_ROPE)], axis=3)
    wq4 = wq4.reshape(d, Q_LORA, QW)
    wq4 = jnp.concatenate([wq4, zeros(d, Q_LORA_PAD - Q_LORA, QW)], axis=1).astype(BF16)
    ukv = w_ukv.reshape(d, KV_LORA, MLA_HEADS, QK_NOPE + V_DIM)
    wk4 = jnp.concatenate([zeros(d, KV_LORA, MLA_HEADS, QK_ROPE), ukv[..., :QK_NOPE],
                           zeros(d, KV_LORA, MLA_HEADS, HEAD_PAD - QK_NOPE - QK_ROPE)], axis=3)
    wk4 = wk4.reshape(d, KV_LORA, QW).astype(BF16)
    wvt = ukv[..., QK_NOPE:].reshape(d, KV_LORA, VW).transpose(0, 2, 1).astype(BF16)
    g_q = jnp.concatenate([g_q_lora, zeros(d, Q_LORA_PAD - Q_LORA)], axis=1)
    return dict(
        g_pre_mix=g_pre_mix.reshape(d, 1, D_MODEL), g_post_mix=g_post_mix.reshape(d, 1, D_MODEL),
        g_pre_ffn=g_pre_ffn.reshape(d, 1, D_MODEL), g_post_ffn=g_post_ffn.reshape(d, 1, D_MODEL),
        w_main=w_in[:, :, :P_MAIN].astype(BF16),
        w_cq=lane_pad(w_in[:, :, P_MAIN:c_kv0], Q_LORA_PAD),
        w_ckv=w_in[:, :, c_kv0:c_kr0].astype(BF16),
        w_kr=lane_pad(w_in[:, :, c_kr0:], LANES),
        wq4=wq4, wk4=wk4, wvt=wvt,
        g_q=g_q.reshape(d, 1, Q_LORA_PAD), g_kv=g_kv_lora.reshape(d, 1, KV_LORA),
        ws=spat_w.reshape(d, A_GROUPS * CHUNK, CHUNK).astype(BF16),
        sb=jnp.repeat(spat_b.transpose(0, 2, 1), GROUP_W // A_GROUPS, axis=2),
        conv_w=conv_w, conv_b=conv_b.reshape(d, 1, GROUP_W),
        w_out=w_out.astype(BF16), w_gate_up=w_gate_up.astype(BF16), w_down=w_down.astype(BF16),
        bd=jnp.broadcast_to(_channel_dft(), (d, GROUP_W, 2 * GROUP_W)),
    )


def _trunk_layer(x, mod_l, lw, layer, lat, nb, seq_len, dft, rope_tabs, cache):
    outs = _premix(x, mod_l, lw, layer, lat, seq_len, rope_tabs)
    pa, pb, a, b, q, k, vt = outs[:7]
    yab = _seqmix(pa, pb, lw, layer, seq_len)
    yc = _posdft(dft[0], dft[1], a, b)
    yd = _attention(q, k, vt, nb, seq_len, cache, layer)
    x = _post(x, mod_l, yab, yc, yd, lw, layer, lat, seq_len)
    return x, outs[7:]


def kernel(x_prompt, x_sample, cache_ckv, cache_krope, c, c_ctx, w_ada, b_ada, g_pre_mix, g_post_mix, g_pre_ffn,
           g_post_ffn, w_in, spat_w, spat_b, conv_w, conv_b, g_q_lora, w_uq, g_kv_lora, w_ukv, w_out, w_gate_up,
           w_down):
    batch, seq, _ = x_prompt.shape
    dec_batch, dec_seq, _ = x_sample.shape
    assert (batch * seq) % TM == 0 and TM % seq == 0 and dec_seq % TM == 0 and dec_batch + 1 <= 8

    stacked = _layout_weights(g_pre_mix, g_post_mix, g_pre_ffn, g_post_ffn, w_in, spat_w, spat_b, conv_w, conv_b,
                              g_q_lora, w_uq, g_kv_lora, w_ukv, w_out, w_gate_up, w_down)
    dft_ctx = _dft_tables(seq)
    dft_lat = _dft_tables(dec_seq)
    rope_tabs = _rope_tables(dec_seq)

    cond8 = jnp.concatenate([c_ctx[None, :], c, jnp.zeros((8 - 1 - dec_batch, D_MODEL), F32)], axis=0)
    mod = _modulation(cond8, w_ada, b_ada).reshape(DEPTH, 8, 6, D_MODEL)

    cache_kr_pad = jnp.pad(cache_krope, ((0, 0), (0, 0), (0, 0), (0, LANES - QK_ROPE)))
    cache = _cache_kv(cache_ckv, cache_kr_pad, stacked["wk4"], stacked["wvt"])

    y_ctx = x_prompt.reshape(batch * seq, D_MODEL)
    y_lat = x_sample.reshape(dec_batch * dec_seq, D_MODEL)
    ckv_list, krope_list = [], []
    for l in range(DEPTH):
        y_ctx, (ckv_l, kr_l) = _trunk_layer(y_ctx, mod[l], stacked, l, False, batch, seq, dft_ctx, None, None)
        ckv_list.append(ckv_l.reshape(batch, seq, KV_LORA))
        krope_list.append(kr_l.reshape(batch, seq, QK_ROPE))
        y_lat, _ = _trunk_layer(y_lat, mod[l], stacked, l, True, dec_batch, dec_seq, dft_lat, rope_tabs, cache)
    state_ckv = jnp.stack(ckv_list, axis=1)
    state_krope = jnp.stack(krope_list, axis=1)
    return (y_ctx.reshape(batch, seq, D_MODEL), y_lat.reshape(dec_batch, dec_seq, D_MODEL), state_ckv, state_krope)
```

```python
import functools
import math

import jax
import jax.numpy as jnp
from jax import lax
from jax.experimental import pallas as pl
from jax.experimental.pallas import tpu as pltpu

F32 = jnp.float32
BF16 = jnp.bfloat16

D_MODEL = 1024
DEPTH = 4
GRID_W = 64
GROUP_W = 256
A_GROUPS = 4
CHUNK = 128
C_GROUPS = 4
C_CH = 64
MLA_HEADS = 4
QK_NOPE = 64
QK_ROPE = 32
V_DIM = 64
Q_LORA = 192
KV_LORA = 128
ROPE_BASE = 10000.0
AXIS_ROPE = QK_ROPE // 2
FF_HIDDEN = 2816
EPS = 1e-6

LANES = 128
HEAD_PAD = 128
Q_LORA_PAD = 256
P_MAIN = 512 + 768 + 256
QW = MLA_HEADS * HEAD_PAD
VW = MLA_HEADS * V_DIM
FF_CHUNK = 256
N_FF_CHUNKS = FF_HIDDEN // FF_CHUNK
TM = 512
TQ = 512
ATT_SUB = 256
ATT_KC = 512
ATT_EB = 128
DFT_SUB = 512
DFT_RB = 32
VMEM_LIMIT = 56 * 1024 * 1024
Q_SCALE = (QK_NOPE + QK_ROPE) ** -0.5 * math.log2(math.e)


def _cparams(sem, **flags):
    return pltpu.CompilerParams(dimension_semantics=sem, vmem_limit_bytes=VMEM_LIMIT, flags=flags or None)


def _layer_spec(w, layer, **kw):
    tail = w.shape[1:]
    nz = (0,) * len(tail)
    return pl.BlockSpec((None,) + tail, lambda *_: (layer,) + nz, **kw)


def _rms_scale(x, n):
    ms = jnp.sum(x * x, axis=-1, keepdims=True) * (1.0 / n)
    return x * lax.rsqrt(ms + EPS)


def _silu(x):
    return x * jax.nn.sigmoid(x)


def _dot(a, b):
    return jnp.dot(a, b, preferred_element_type=F32)


def _dot_nt(a, b):
    return lax.dot_general(a, b, (((1,), (1,)), ((), ())), preferred_element_type=F32)


def _mod_kernel(cond_ref, w_ref, b_ref, o_ref):
    s = _silu(cond_ref[...])
    o_ref[...] = _dot(s.astype(BF16), w_ref[...].astype(BF16)) + b_ref[...]


def _modulation(cond8, w_ada, b_ada):
    nt = 6 * D_MODEL // 1024
    return pl.pallas_call(
        _mod_kernel,
        grid=(DEPTH, nt),
        in_specs=[
            pl.BlockSpec((8, D_MODEL), lambda l, j: (0, 0)),
            pl.BlockSpec((None, D_MODEL, 1024), lambda l, j: (l, 0, j)),
            pl.BlockSpec((None, 1, 1024), lambda l, j: (l, 0, j)),
        ],
        out_specs=pl.BlockSpec((None, 8, 1024), lambda l, j: (l, 0, j)),
        out_shape=jax.ShapeDtypeStruct((DEPTH, 8, 6 * D_MODEL), F32),
        compiler_params=_cparams(("arbitrary", "arbitrary")),
        name="modulation",
    )(cond8, w_ada, b_ada.reshape(DEPTH, 1, 6 * D_MODEL))


def _cachekv_kernel(ckv_ref, kr_ref, wk_ref, wvt_ref, k_ref, vt_ref):
    c = ckv_ref[...].astype(BF16)
    kk = _dot(c, wk_ref[...])
    kr = kr_ref[...]
    for h in range(MLA_HEADS):
        k_ref[:, h * HEAD_PAD:(h + 1) * HEAD_PAD] = (kk[:, h * HEAD_PAD:(h + 1) * HEAD_PAD] + kr).astype(BF16)
    vt_ref[...] = _dot_nt(wvt_ref[...], c).astype(BF16)


def _cache_kv(cache_ckv, cache_kr_pad, wk4, wvt):
    nb, _, past, _ = cache_ckv.shape
    return pl.pallas_call(
        _cachekv_kernel,
        grid=(nb, DEPTH),
        in_specs=[
            pl.BlockSpec((None, None, past, KV_LORA), lambda b, l: (b, l, 0, 0)),
            pl.BlockSpec((None, None, past, LANES), lambda b, l: (b, l, 0, 0)),
            pl.BlockSpec((None, KV_LORA, QW), lambda b, l: (l, 0, 0)),
            pl.BlockSpec((None, VW, KV_LORA), lambda b, l: (l, 0, 0)),
        ],
        out_specs=[
            pl.BlockSpec((None, None, past, QW), lambda b, l: (b, l, 0, 0)),
            pl.BlockSpec((None, None, VW, past), lambda b, l: (b, l, 0, 0)),
        ],
        out_shape=[
            jax.ShapeDtypeStruct((nb, DEPTH, past, QW), BF16),
            jax.ShapeDtypeStruct((nb, DEPTH, VW, past), BF16),
        ],
        compiler_params=_cparams(("arbitrary", "arbitrary")),
        name="cache_kv",
    )(cache_ckv, cache_kr_pad, wk4, wvt)


def _rope(x, cos, sin):
    w = x.shape[1]
    lane = lax.broadcasted_iota(jnp.int32, x.shape, 1)
    up = pltpu.roll(x, w - 8, axis=1)
    dn = pltpu.roll(x, 8, axis=1)
    sw = jnp.where((lane & 15) < 8, up, dn)
    reps = w // LANES
    if reps > 1:
        cos = jnp.concatenate([cos] * reps, axis=1)
        sin = jnp.concatenate([sin] * reps, axis=1)
    return x * cos + sw * sin


def _premix_kernel(*refs, lat, seq_len):
    if lat:
        (x_ref, mod_ref, g_ref, win_ref, wcq_ref, wckv_ref, wkr_ref, bd_ref, gq_ref, wq_ref, gkv_ref, wk_ref,
         wvt_ref, cos_ref, sin_ref, pa_ref, pb_ref, a_ref, b_ref, q_ref, k_ref, vt_ref) = refs
    else:
        (x_ref, mod_ref, g_ref, win_ref, wcq_ref, wckv_ref, wkr_ref, bd_ref, gq_ref, wq_ref, gkv_ref, wk_ref,
         wvt_ref, dc_ref, ds_ref, pa_ref, pb_ref, yc_ref, q_ref, k_ref, vt_ref, ckv_ref, kr_ref) = refs
    gain = g_ref[...] * (1.0 + mod_ref[1:2, :])
    h = _rms_scale(x_ref[...], D_MODEL) * gain + mod_ref[0:1, :]
    hb = h.astype(BF16)
    pa_ref[...] = _dot(hb, win_ref[:, 0:512]).astype(BF16)
    pb_ref[...] = _dot(hb, win_ref[:, 512:1280]).astype(BF16)
    pc = _dot(hb, win_ref[:, 1280:P_MAIN]).astype(BF16)
    ab = _dot(pc, bd_ref[...])
    if lat:
        a_ref[...] = ab[:, 0:GROUP_W]
        b_ref[...] = ab[:, GROUP_W:2 * GROUP_W]
    else:
        abb = ab.astype(BF16)
        for s in range(TM // seq_len):
            rows = slice(s * seq_len, (s + 1) * seq_len)
            yc_ref[rows, :] = (_dot(dc_ref[...], abb[rows, 0:GROUP_W])
                               + _dot(ds_ref[...], abb[rows, GROUP_W:2 * GROUP_W])).astype(BF16)
    cq = _dot(hb, wcq_ref[...])
    ckv = _dot(hb, wckv_ref[...])
    kr = _dot(hb, wkr_ref[...])
    qn = _rms_scale(cq, Q_LORA) * gq_ref[...]
    q4 = _dot(qn.astype(BF16), wq_ref[...])
    ckv_n = _rms_scale(ckv, KV_LORA) * gkv_ref[...]
    cb = ckv_n.astype(BF16)
    kk = _dot(cb, wk_ref[...])
    vt = _dot_nt(wvt_ref[...], cb).astype(BF16)
    if lat:
        vt_ref[...] = vt
        q4 = _rope(q4, cos_ref[...], sin_ref[...])
        kr = _rope(kr, cos_ref[...], sin_ref[...])
    else:
        for s in range(TM // seq_len):
            vt_ref[s] = vt[:, s * seq_len:(s + 1) * seq_len]
        ckv_ref[...] = ckv_n
        kr_ref[...] = kr[:, 0:QK_ROPE]
    q_ref[...] = (q4 * Q_SCALE).astype(BF16)
    for hd in range(MLA_HEADS):
        k_ref[:, hd * HEAD_PAD:(hd + 1) * HEAD_PAD] = (kk[:, hd * HEAD_PAD:(hd + 1) * HEAD_PAD] + kr).astype(BF16)


def _premix(x, mod_l, lw, layer, lat, seq_len, tabs):
    t = x.shape[0]
    nt = t // TM
    nb = t // seq_len
    tpb = max(seq_len // TM, 1)
    row = lambda i: (i, 0)
    if lat:
        mod_map = lambda i: (1 + i // tpb, 0, 0)
        vt_spec = pl.BlockSpec((None, VW, TM), lambda i: (i // tpb, 0, i % tpb))
        tab_specs = [pl.BlockSpec((TM, LANES), lambda i: (i % tpb, 0))] * 2
        fourier_specs = [pl.BlockSpec((TM, GROUP_W), row)] * 2
        fourier_shapes = [jax.ShapeDtypeStruct((t, GROUP_W), F32)] * 2
    else:
        mod_map = lambda i: (0, 0, 0)
        vt_spec = pl.BlockSpec((TM // seq_len, VW, seq_len), lambda i: (i, 0, 0))
        tab_specs = [pl.BlockSpec((seq_len, seq_len), lambda i: (0, 0))] * 2
        fourier_specs = [pl.BlockSpec((TM, GROUP_W), row)]
        fourier_shapes = [jax.ShapeDtypeStruct((t, GROUP_W), BF16)]
    names = ["g_pre_mix", "w_main", "w_cq", "w_ckv", "w_kr", "bd", "g_q", "wq4", "g_kv", "wk4", "wvt"]
    in_specs = [pl.BlockSpec((TM, D_MODEL), row), pl.BlockSpec((None, 6, D_MODEL), mod_map)]
    in_specs += [_layer_spec(lw[n], layer) for n in names] + tab_specs
    args = [x, mod_l] + [lw[n] for n in names] + list(tabs)
    out_specs = [pl.BlockSpec((TM, 512), row), pl.BlockSpec((TM, 768), row)] + fourier_specs + [
        pl.BlockSpec((TM, QW), row),
        pl.BlockSpec((TM, QW), row),
        vt_spec,
    ]
    out_shape = [jax.ShapeDtypeStruct((t, 512), BF16), jax.ShapeDtypeStruct((t, 768), BF16)] + fourier_shapes + [
        jax.ShapeDtypeStruct((t, QW), BF16),
        jax.ShapeDtypeStruct((t, QW), BF16),
        jax.ShapeDtypeStruct((nb, VW, seq_len), BF16),
    ]
    if not lat:
        out_specs += [pl.BlockSpec((TM, KV_LORA), row), pl.BlockSpec((TM, QK_ROPE), row)]
        out_shape += [jax.ShapeDtypeStruct((t, KV_LORA), F32), jax.ShapeDtypeStruct((t, QK_ROPE), F32)]
    return pl.pallas_call(
        functools.partial(_premix_kernel, lat=lat, seq_len=seq_len),
        grid=(nt,),
        in_specs=in_specs,
        out_specs=out_specs,
        out_shape=out_shape,
        compiler_params=_cparams(("arbitrary",)),
        name="premix_lat" if lat else "premix_ctx",
    )(*args)


def _seqmix_kernel(pa_ref, pb_ref, prev_ref, next_ref, ws_ref, sb_ref, cw_ref, cb_ref, o_ref, *, seq_len):
    i = pl.program_id(0)
    grp = lax.broadcasted_iota(jnp.int32, (CHUNK, GROUP_W), 1) // (GROUP_W // A_GROUPS)
    r = lax.broadcasted_iota(jnp.int32, (CHUNK, GROUP_W), 0)

    def z_row(ref, row):
        blk = slice(row // 16 * 16, row // 16 * 16 + 16)
        zz = ref[blk, 2 * GROUP_W:3 * GROUP_W].astype(F32) * ref[blk, 0:GROUP_W].astype(F32)
        return zz[row % 16:row % 16 + 1]

    tile_pos = (i * TM) & (seq_len - 1)
    for c in range(TM // CHUNK):
        rows = slice(c * CHUNK, (c + 1) * CHUNK)
        u = pa_ref[rows, 0:GROUP_W].astype(F32)
        v = pa_ref[rows, GROUP_W:2 * GROUP_W]
        m4 = _dot(ws_ref[...], v)
        mixed = m4[0:CHUNK]
        for g in range(1, A_GROUPS):
            mixed = jnp.where(grp == g, m4[g * CHUNK:(g + 1) * CHUNK], mixed)
        o_ref[rows, 0:GROUP_W] = (u * (mixed + sb_ref[...])).astype(BF16)
        z = pb_ref[rows, 2 * GROUP_W:3 * GROUP_W].astype(F32) * pb_ref[rows, 0:GROUP_W].astype(F32)
        if c > 0:
            before = z_row(pb_ref, c * CHUNK - 1) if (c * CHUNK) % seq_len else jnp.zeros((1, GROUP_W), F32)
        else:
            before = jnp.where(tile_pos != 0, z_row(prev_ref, 15), 0.0)
        if c < TM // CHUNK - 1:
            after = z_row(pb_ref, (c + 1) * CHUNK) if ((c + 1) * CHUNK) % seq_len else jnp.zeros((1, GROUP_W), F32)
        else:
            after = jnp.where(((tile_pos + TM) & (seq_len - 1)) != 0, z_row(next_ref, 0), 0.0)
        z_prev = jnp.where(r == 0, before, pltpu.roll(z, 1, axis=0))
        z_next = jnp.where(r == CHUNK - 1, after, pltpu.roll(z, CHUNK - 1, axis=0))
        y = z_prev * cw_ref[0:1, :] + z * cw_ref[1:2, :] + z_next * cw_ref[2:3, :] + cb_ref[...]
        gb = pb_ref[rows, GROUP_W:2 * GROUP_W].astype(F32)
        o_ref[rows, GROUP_W:2 * GROUP_W] = (gb * y).astype(BF16)


def _seqmix(pa, pb, lw, layer, seq_len):
    t = pa.shape[0]
    nt = t // TM
    hb = TM // 16
    last = t // 16 - 1
    row = lambda i: (i, 0)
    names = ["ws", "sb", "conv_w", "conv_b"]
    return pl.pallas_call(
        functools.partial(_seqmix_kernel, seq_len=seq_len),
        grid=(nt,),
        in_specs=[
            pl.BlockSpec((TM, 512), row),
            pl.BlockSpec((TM, 768), row),
            pl.BlockSpec((16, 768), lambda i: (jnp.maximum(i * hb - 1, 0), 0)),
            pl.BlockSpec((16, 768), lambda i: (jnp.minimum((i + 1) * hb, last), 0)),
        ] + [_layer_spec(lw[n], layer) for n in names],
        out_specs=pl.BlockSpec((TM, 2 * GROUP_W), row),
        out_shape=jax.ShapeDtypeStruct((t, 2 * GROUP_W), BF16),
        compiler_params=_cparams(("arbitrary",)),
        name="seqmix",
    )(pa, pb, pb, pb, *[lw[n] for n in names])


def _posdft_kernel(a_ref, b_ref, twc_ref, tws_ref, dc_ref, ds_ref, o_ref, x_scr, y_scr, *, levels):
    n = a_ref.shape[0]
    tw0 = 0
    for lv in range(levels):
        size = n >> lv
        half = size // 2
        src_x, src_y = (a_ref, b_ref) if lv == 0 else (x_scr, y_scr)
        for base in range(0, n, size):
            for j0 in range(0, half, DFT_RB):
                top = slice(base + j0, base + j0 + DFT_RB)
                bot = slice(base + half + j0, base + half + j0 + DFT_RB)
                xt, xb, yt, yb = src_x[top, :], src_x[bot, :], src_y[top, :], src_y[bot, :]
                c = twc_ref[tw0 + j0:tw0 + j0 + DFT_RB, :]
                s = tws_ref[tw0 + j0:tw0 + j0 + DFT_RB, :]
                c = jnp.concatenate([c] * (GROUP_W // LANES), axis=1)
                s = jnp.concatenate([s] * (GROUP_W // LANES), axis=1)
                x_scr[top, :] = xt + xb
                y_scr[top, :] = yt + yb
                dx, dy = xt - xb, yt - yb
                x_scr[bot, :] = dx * c - dy * s
                y_scr[bot, :] = dx * s + dy * c
        tw0 += half
    sub = n >> levels
    for p in range(1 << levels):
        r = int(format(p, "0%db" % levels)[::-1], 2) if levels else 0
        rows = slice(p * sub, (p + 1) * sub)
        xb = x_scr[rows, :].astype(BF16)
        yb = y_scr[rows, :].astype(BF16)
        o_ref[:, r * GROUP_W:(r + 1) * GROUP_W] = (_dot(dc_ref[...], xb) + _dot(ds_ref[...], yb)).astype(BF16)


def _posdft(a, b, tabs, nb, n):
    twc, tws, dc, ds = tabs
    sub = dc.shape[0]
    levels = (n // sub).bit_length() - 1
    const = lambda i: (0, 0)
    out = pl.pallas_call(
        functools.partial(_posdft_kernel, levels=levels),
        grid=(nb,),
        in_specs=[
            pl.BlockSpec((n, GROUP_W), lambda i: (i, 0)),
            pl.BlockSpec((n, GROUP_W), lambda i: (i, 0)),
            pl.BlockSpec(twc.shape, const),
            pl.BlockSpec(tws.shape, const),
            pl.BlockSpec((sub, sub), const),
            pl.BlockSpec((sub, sub), const),
        ],
        out_specs=pl.BlockSpec((None, sub, (n // sub) * GROUP_W), lambda i: (i, 0, 0)),
        out_shape=jax.ShapeDtypeStruct((nb, sub, (n // sub) * GROUP_W), BF16),
        scratch_shapes=[pltpu.VMEM((n, GROUP_W), F32), pltpu.VMEM((n, GROUP_W), F32)],
        compiler_params=_cparams(("arbitrary",)),
        name="posdft",
    )(a, b, twc, tws, dc, ds)
    return out.reshape(nb * n, GROUP_W)


def _attn_kernel(*refs, has_cache):
    if has_cache:
        q_ref, k_ref, vt_ref, kc_ref, vct_ref, o_ref, st_scr, pt_scr = refs
    else:
        q_ref, k_ref, vt_ref, o_ref, st_scr, pt_scr = refs
    nk = k_ref.shape[0]
    kc = min(nk, ATT_KC)
    segs = [(k_ref, vt_ref, c * kc, c * kc, kc) for c in range(nk // kc)]
    if has_cache:
        segs.append((kc_ref, vct_ref, 0, nk, kc_ref.shape[0]))
    nsub = q_ref.shape[0] // ATT_SUB
    items = [(s, h) for s in range(nsub) for h in range(MLA_HEADS)]
    outs = [[None] * MLA_HEADS for _ in range(nsub)]
    m_prev = None
    for t in range(len(items) + 1):
        if t < len(items):
            s, h = items[t]
            cols = slice(h * HEAD_PAD, (h + 1) * HEAD_PAD)
            qh = q_ref[s * ATT_SUB:(s + 1) * ATT_SUB, cols]
            m = None
        if t >= 1:
            ps, ph = items[t - 1]
            vrows = slice(ph * V_DIM, (ph + 1) * V_DIM)
            l = jnp.zeros((1, ATT_SUB), F32)
            ot = jnp.zeros((V_DIM, ATT_SUB), F32)
        for ci, (kref, vref, r0, s0, n) in enumerate(segs):
            if t < len(items):
                st_scr[t % 2, s0:s0 + n, :] = _dot_nt(kref[r0:r0 + n, cols], qh)
                for j in range(0, n, ATT_EB):
                    cm = jnp.max(st_scr[t % 2, s0 + j:s0 + j + ATT_EB, :], axis=0, keepdims=True)
                    m = cm if m is None else jnp.maximum(m, cm)
            if t >= 1:
                for j in range(0, n, ATT_EB):
                    p = jnp.exp2(st_scr[(t - 1) % 2, s0 + j:s0 + j + ATT_EB, :] - m_prev)
                    l = l + jnp.sum(p, axis=0, keepdims=True)
                    pt_scr[ci % 2, j:j + ATT_EB, :] = p.astype(BF16)
                ot = ot + _dot(vref[vrows, r0:r0 + n], pt_scr[ci % 2, 0:n, :])
        if t >= 1:
            outs[ps][ph] = ot / l
        if t < len(items):
            m_prev = m
    for s in range(nsub):
        o_ref[s * ATT_SUB:(s + 1) * ATT_SUB, :] = jnp.concatenate(outs[s], axis=0).T.astype(BF16)


def _attention(q, k, vt, nb, seq_len, cache=None, layer=0):
    tq = min(seq_len, TQ)
    nk_total = seq_len + (cache[0].shape[2] if cache is not None else 0)
    q3 = q.reshape(nb, seq_len, QW)
    k3 = k.reshape(nb, seq_len, QW)
    in_specs = [
        pl.BlockSpec((None, tq, QW), lambda b, i: (b, i, 0)),
        pl.BlockSpec((None, seq_len, QW), lambda b, i: (b, 0, 0)),
        pl.BlockSpec((None, VW, seq_len), lambda b, i: (b, 0, 0)),
    ]
    args = [q3, k3, vt]
    if cache is not None:
        kc, vct = cache
        past = kc.shape[2]
        in_specs += [
            pl.BlockSpec((None, None, past, QW), lambda b, i: (b, layer, 0, 0)),
            pl.BlockSpec((None, None, VW, past), lambda b, i: (b, layer, 0, 0)),
        ]
        args += [kc, vct]
    out = pl.pallas_call(
        functools.partial(_attn_kernel, has_cache=cache is not None),
        grid=(nb, seq_len // tq),
        in_specs=in_specs,
        out_specs=pl.BlockSpec((None, tq, VW), lambda b, i: (b, i, 0)),
        out_shape=jax.ShapeDtypeStruct((nb, seq_len, VW), BF16),
        scratch_shapes=[pltpu.VMEM((2, nk_total, ATT_SUB), F32),
                        pltpu.VMEM((2, min(seq_len, ATT_KC), ATT_SUB), BF16)],
        compiler_params=_cparams(("arbitrary", "arbitrary")),
        name="attn_lat" if cache is not None else "attn_ctx",
    )(*args)
    return out.reshape(nb * seq_len, VW)


def _post_kernel(x_ref, mod_ref, yab_ref, yc_ref, yd_ref, wo_ref, gpm_ref, gpf_ref, gpo_ref,
                 wgu_ref, wd_ref, o_ref, x1_scr, acc_scr):
    mix = (_dot(yab_ref[...], wo_ref[0:2 * GROUP_W, :]) + _dot(yc_ref[...], wo_ref[2 * GROUP_W:3 * GROUP_W, :])
           + _dot(yd_ref[...], wo_ref[3 * GROUP_W:4 * GROUP_W, :]))
    x1 = x_ref[...] + _rms_scale(mix, D_MODEL) * (mod_ref[2:3, :] * gpm_ref[...])
    x1_scr[...] = x1
    h2 = _rms_scale(x1, D_MODEL) * (gpf_ref[...] * (1.0 + mod_ref[4:5, :])) + mod_ref[3:4, :]
    hb = h2.astype(BF16)
    for c in range(N_FF_CHUNKS):
        g = _dot(hb, wgu_ref[:, c * FF_CHUNK:(c + 1) * FF_CHUNK])
        u = _dot(hb, wgu_ref[:, FF_HIDDEN + c * FF_CHUNK:FF_HIDDEN + (c + 1) * FF_CHUNK])
        act = (_silu(g) * u).astype(BF16)
        d = _dot(act, wd_ref[c * FF_CHUNK:(c + 1) * FF_CHUNK, :])
        if c == 0:
            acc_scr[...] = d
        else:
            acc_scr[...] += d
    o_ref[...] = x1_scr[...] + _rms_scale(acc_scr[...], D_MODEL) * (mod_ref[5:6, :] * gpo_ref[...])


def _post(x, mod_l, yab, yc, yd, lw, layer, lat, seq_len):
    t = x.shape[0]
    nt = t // TM
    tpb = max(seq_len // TM, 1)
    mod_map = (lambda i: (1 + i // tpb, 0, 0)) if lat else (lambda i: (0, 0, 0))
    row = lambda i: (i, 0)
    resident = dict(pipeline_mode=pl.Buffered(1))
    return pl.pallas_call(
        _post_kernel,
        grid=(nt,),
        in_specs=[
            pl.BlockSpec((TM, D_MODEL), row),
            pl.BlockSpec((None, 6, D_MODEL), mod_map),
            pl.BlockSpec((TM, 2 * GROUP_W), row),
            pl.BlockSpec((TM, GROUP_W), row),
            pl.BlockSpec((TM, GROUP_W), row),
            _layer_spec(lw["w_out"], layer, **resident),
            _layer_spec(lw["g_post_mix"], layer),
            _layer_spec(lw["g_pre_ffn"], layer),
            _layer_spec(lw["g_post_ffn"], layer),
            _layer_spec(lw["w_gate_up"], layer, **resident),
            _layer_spec(lw["w_down"], layer, **resident),
        ],
        out_specs=pl.BlockSpec((TM, D_MODEL), row),
        out_shape=jax.ShapeDtypeStruct((t, D_MODEL), F32),
        scratch_shapes=[
            pltpu.VMEM((TM, D_MODEL), F32),
            pltpu.VMEM((TM, D_MODEL), F32),
        ],
        compiler_params=_cparams(("arbitrary",)),
        name="post_lat" if lat else "post_ctx",
    )(x, mod_l, yab, yc, yd, lw["w_out"], lw["g_post_mix"], lw["g_pre_ffn"], lw["g_post_ffn"],
      lw["w_gate_up"], lw["w_down"])


def _dft_tables(n, seq_len):
    scale = seq_len ** -0.5
    k = jnp.arange(n, dtype=jnp.int32)
    ang = ((k[:, None] * k[None, :]) % n).astype(F32) * (2.0 * math.pi / n)
    return (jnp.cos(ang) * scale).astype(BF16), (-jnp.sin(ang) * scale).astype(BF16)


def _twiddle_tables(n, sub):
    angs = []
    size = n
    while size > sub:
        angs.append(jnp.arange(size // 2, dtype=F32) * (2.0 * math.pi / size))
        size //= 2
    ang = jnp.concatenate(angs)[:, None]
    return jnp.broadcast_to(jnp.cos(ang), (ang.shape[0], LANES)), jnp.broadcast_to(jnp.sin(ang), (ang.shape[0], LANES))


def _channel_dft():
    c = jnp.arange(C_CH, dtype=jnp.int32)
    ang = ((c[:, None] * c[None, :]) % C_CH).astype(F32) * (2.0 * math.pi / C_CH)
    eye = jnp.eye(C_GROUPS, dtype=F32)
    bc = jnp.kron(eye, jnp.cos(ang) * C_CH ** -0.5)
    bs = jnp.kron(eye, jnp.sin(ang) * C_CH ** -0.5)
    return jnp.concatenate([bc, bs], axis=1).astype(BF16)


def _rope_tables(n):
    pos = jnp.arange(n, dtype=jnp.int32)
    row = (pos // GRID_W).astype(F32)
    col = (pos % GRID_W).astype(F32)
    inv = ROPE_BASE ** (-jnp.arange(0, AXIS_ROPE, 2, dtype=F32) / AXIS_ROPE)
    half = AXIS_ROPE // 2
    ang = jnp.concatenate([row[:, None] * inv, row[:, None] * inv, col[:, None] * inv, col[:, None] * inv], axis=1)
    sign = jnp.tile(jnp.concatenate([-jnp.ones((half,), F32), jnp.ones((half,), F32)]), 2)
    cos = jnp.concatenate([jnp.cos(ang), jnp.ones((n, LANES - QK_ROPE), F32)], axis=1)
    sin = jnp.concatenate([jnp.sin(ang) * sign, jnp.zeros((n, LANES - QK_ROPE), F32)], axis=1)
    return cos, sin


def _layout_weights(g_pre_mix, g_post_mix, g_pre_ffn, g_post_ffn, w_in, spat_w, spat_b, conv_w, conv_b,
                    g_q_lora, w_uq, g_kv_lora, w_ukv, w_out, w_gate_up, w_down):
    d = DEPTH
    zeros = lambda *s: jnp.zeros(s, F32)
    c_kv0 = P_MAIN + Q_LORA
    c_kr0 = c_kv0 + KV_LORA
    lane_pad = lambda w, width: jnp.pad(w, ((0, 0), (0, 0), (0, width - w.shape[2]))).astype(BF16)
    uq = w_uq.reshape(d, Q_LORA, MLA_HEADS, QK_NOPE + QK_ROPE)
    wq4 = jnp.concatenate([uq[..., QK_NOPE:], uq[..., :QK_NOPE],
                           zeros(d, Q_LORA, MLA_HEADS, HEAD_PAD - QK_NOPE - QK_ROPE)], axis=3)
    wq4 = wq4.reshape(d, Q_LORA, QW)
    wq4 = jnp.concatenate([wq4, zeros(d, Q_LORA_PAD - Q_LORA, QW)], axis=1).astype(BF16)
    ukv = w_ukv.reshape(d, KV_LORA, MLA_HEADS, QK_NOPE + V_DIM)
    wk4 = jnp.concatenate([zeros(d, KV_LORA, MLA_HEADS, QK_ROPE), ukv[..., :QK_NOPE],
                           zeros(d, KV_LORA, MLA_HEADS, HEAD_PAD - QK_NOPE - QK_ROPE)], axis=3)
    wk4 = wk4.reshape(d, KV_LORA, QW).astype(BF16)
    wvt = ukv[..., QK_NOPE:].reshape(d, KV_LORA, VW).transpose(0, 2, 1).astype(BF16)
    g_q = jnp.concatenate([g_q_lora, zeros(d, Q_LORA_PAD - Q_LORA)], axis=1)
    return dict(
        g_pre_mix=g_pre_mix.reshape(d, 1, D_MODEL), g_post_mix=g_post_mix.reshape(d, 1, D_MODEL),
        g_pre_ffn=g_pre_ffn.reshape(d, 1, D_MODEL), g_post_ffn=g_post_ffn.reshape(d, 1, D_MODEL),
        w_main=w_in[:, :, :P_MAIN].astype(BF16),
        w_cq=lane_pad(w_in[:, :, P_MAIN:c_kv0], Q_LORA_PAD),
        w_ckv=w_in[:, :, c_kv0:c_kr0].astype(BF16),
        w_kr=lane_pad(w_in[:, :, c_kr0:], LANES),
        wq4=wq4, wk4=wk4, wvt=wvt,
        g_q=g_q.reshape(d, 1, Q_LORA_PAD), g_kv=g_kv_lora.reshape(d, 1, KV_LORA),
        ws=spat_w.reshape(d, A_GROUPS * CHUNK, CHUNK).astype(BF16),
        sb=jnp.repeat(spat_b.transpose(0, 2, 1), GROUP_W // A_GROUPS, axis=2),
        conv_w=conv_w, conv_b=conv_b.reshape(d, 1, GROUP_W),
        w_out=w_out.astype(BF16), w_gate_up=w_gate_up.astype(BF16), w_down=w_down.astype(BF16),
        bd=jnp.broadcast_to(_channel_dft(), (d, GROUP_W, 2 * GROUP_W)),
    )


def _trunk_layer(x, mod_l, lw, layer, lat, nb, seq_len, premix_tabs, dft_tabs, cache):
    outs = _premix(x, mod_l, lw, layer, lat, seq_len, premix_tabs)
    if lat:
        pa, pb, a, b, q, k, vt = outs
        yc = _posdft(a, b, dft_tabs, nb, seq_len)
        states = ()
    else:
        pa, pb, yc, q, k, vt = outs[:6]
        states = outs[6:]
    yab = _seqmix(pa, pb, lw, layer, seq_len)
    yd = _attention(q, k, vt, nb, seq_len, cache, layer)
    x = _post(x, mod_l, yab, yc, yd, lw, layer, lat, seq_len)
    return x, states


def kernel(x_prompt, x_sample, cache_ckv, cache_krope, c, c_ctx, w_ada, b_ada, g_pre_mix, g_post_mix, g_pre_ffn,
           g_post_ffn, w_in, spat_w, spat_b, conv_w, conv_b, g_q_lora, w_uq, g_kv_lora, w_ukv, w_out, w_gate_up,
           w_down):
    batch, seq, _ = x_prompt.shape
    dec_batch, dec_seq, _ = x_sample.shape
    assert (batch * seq) % TM == 0 and TM % seq == 0 and dec_seq % TM == 0 and dec_batch + 1 <= 8

    stacked = _layout_weights(g_pre_mix, g_post_mix, g_pre_ffn, g_post_ffn, w_in, spat_w, spat_b, conv_w, conv_b,
                              g_q_lora, w_uq, g_kv_lora, w_ukv, w_out, w_gate_up, w_down)
    dft_ctx = _dft_tables(seq, seq)
    dft_sub = min(dec_seq, DFT_SUB)
    dft_lat = _twiddle_tables(dec_seq, dft_sub) + _dft_tables(dft_sub, dec_seq)
    rope_tabs = _rope_tables(dec_seq)

    cond8 = jnp.concatenate([c_ctx[None, :], c, jnp.zeros((8 - 1 - dec_batch, D_MODEL), F32)], axis=0)
    mod = _modulation(cond8, w_ada, b_ada).reshape(DEPTH, 8, 6, D_MODEL)

    cache_kr_pad = jnp.pad(cache_krope, ((0, 0), (0, 0), (0, 0), (0, LANES - QK_ROPE)))
    cache = _cache_kv(cache_ckv, cache_kr_pad, stacked["wk4"], stacked["wvt"])

    y_ctx = x_prompt.reshape(batch * seq, D_MODEL)
    y_lat = x_sample.reshape(dec_batch * dec_seq, D_MODEL)
    ckv_list, krope_list = [], []
    for l in range(DEPTH):
        y_ctx, (ckv_l, kr_l) = _trunk_layer(y_ctx, mod[l], stacked, l, False, batch, seq, dft_ctx, None, None)
        ckv_list.append(ckv_l.reshape(batch, seq, KV_LORA))
        krope_list.append(kr_l.reshape(batch, seq, QK_ROPE))
        y_lat, _ = _trunk_layer(y_lat, mod[l], stacked, l, True, dec_batch, dec_seq, rope_tabs, dft_lat, cache)
    state_ckv = jnp.stack(ckv_list, axis=1)
    state_krope = jnp.stack(krope_list, axis=1)
    return (y_ctx.reshape(batch, seq, D_MODEL), y_lat.reshape(dec_batch, dec_seq, D_MODEL), state_ckv, state_krope)
```

```python
import functools
import math

import jax
import jax.numpy as jnp
from jax import lax
from jax.experimental import pallas as pl
from jax.experimental.pallas import tpu as pltpu

F32 = jnp.float32
BF16 = jnp.bfloat16

D_MODEL = 1024
DEPTH = 4
GRID_W = 64
GROUP_W = 256
A_GROUPS = 4
CHUNK = 128
C_GROUPS = 4
C_CH = 64
MLA_HEADS = 4
QK_NOPE = 64
QK_ROPE = 32
V_DIM = 64
Q_LORA = 192
KV_LORA = 128
ROPE_BASE = 10000.0
AXIS_ROPE = QK_ROPE // 2
FF_HIDDEN = 2816
EPS = 1e-6

LANES = 128
HEAD_PAD = 128
Q_LORA_PAD = 256
P_MAIN = 512 + 768 + 256
QW = MLA_HEADS * HEAD_PAD
VW = MLA_HEADS * V_DIM
FF_CHUNK = 256
N_FF_CHUNKS = FF_HIDDEN // FF_CHUNK
TM = 512
POST_TM = 1024
TQ = 512
ATT_SUB = 256
ATT_KC = 512
ATT_EB = 128
DFT_SUB = 512
DFT_RB = 32
VMEM_LIMIT = 56 * 1024 * 1024
Q_SCALE = (QK_NOPE + QK_ROPE) ** -0.5 * math.log2(math.e)


def _cparams(sem, **flags):
    return pltpu.CompilerParams(dimension_semantics=sem, vmem_limit_bytes=VMEM_LIMIT, flags=flags or None)


def _layer_spec(w, layer, **kw):
    tail = w.shape[1:]
    nz = (0,) * len(tail)
    return pl.BlockSpec((None,) + tail, lambda *_: (layer,) + nz, **kw)


def _rms_scale(x, n):
    ms = jnp.sum(x * x, axis=-1, keepdims=True) * (1.0 / n)
    return x * lax.rsqrt(ms + EPS)


def _silu(x):
    return x * jax.nn.sigmoid(x)


def _dot(a, b):
    return jnp.dot(a, b, preferred_element_type=F32)


def _dot_nt(a, b):
    return lax.dot_general(a, b, (((1,), (1,)), ((), ())), preferred_element_type=F32)


def _mod_kernel(cond_ref, w_ref, b_ref, o_ref):
    s = _silu(cond_ref[...])
    o_ref[...] = _dot(s.astype(BF16), w_ref[...].astype(BF16)) + b_ref[...]


def _modulation(cond8, w_ada, b_ada):
    nt = 6 * D_MODEL // 1024
    return pl.pallas_call(
        _mod_kernel,
        grid=(DEPTH, nt),
        in_specs=[
            pl.BlockSpec((8, D_MODEL), lambda l, j: (0, 0)),
            pl.BlockSpec((None, D_MODEL, 1024), lambda l, j: (l, 0, j)),
            pl.BlockSpec((None, 1, 1024), lambda l, j: (l, 0, j)),
        ],
        out_specs=pl.BlockSpec((None, 8, 1024), lambda l, j: (l, 0, j)),
        out_shape=jax.ShapeDtypeStruct((DEPTH, 8, 6 * D_MODEL), F32),
        compiler_params=_cparams(("arbitrary", "arbitrary")),
        name="modulation",
    )(cond8, w_ada, b_ada.reshape(DEPTH, 1, 6 * D_MODEL))


def _cachekv_kernel(ckv_ref, kr_ref, wk_ref, wvt_ref, k_ref, vt_ref):
    c = ckv_ref[...].astype(BF16)
    kk = _dot(c, wk_ref[...])
    kr = kr_ref[...]
    for h in range(MLA_HEADS):
        k_ref[:, h * HEAD_PAD:(h + 1) * HEAD_PAD] = (kk[:, h * HEAD_PAD:(h + 1) * HEAD_PAD] + kr).astype(BF16)
    vt_ref[...] = _dot_nt(wvt_ref[...], c).astype(BF16)


def _cache_kv(cache_ckv, cache_kr_pad, wk4, wvt):
    nb, _, past, _ = cache_ckv.shape
    return pl.pallas_call(
        _cachekv_kernel,
        grid=(nb, DEPTH),
        in_specs=[
            pl.BlockSpec((None, None, past, KV_LORA), lambda b, l: (b, l, 0, 0)),
            pl.BlockSpec((None, None, past, LANES), lambda b, l: (b, l, 0, 0)),
            pl.BlockSpec((None, KV_LORA, QW), lambda b, l: (l, 0, 0)),
            pl.BlockSpec((None, VW, KV_LORA), lambda b, l: (l, 0, 0)),
        ],
        out_specs=[
            pl.BlockSpec((None, None, past, QW), lambda b, l: (b, l, 0, 0)),
            pl.BlockSpec((None, None, VW, past), lambda b, l: (b, l, 0, 0)),
        ],
        out_shape=[
            jax.ShapeDtypeStruct((nb, DEPTH, past, QW), BF16),
            jax.ShapeDtypeStruct((nb, DEPTH, VW, past), BF16),
        ],
        compiler_params=_cparams(("arbitrary", "arbitrary")),
        name="cache_kv",
    )(cache_ckv, cache_kr_pad, wk4, wvt)


def _rope(x, cos, sin):
    w = x.shape[1]
    lane = lax.broadcasted_iota(jnp.int32, x.shape, 1)
    up = pltpu.roll(x, w - 8, axis=1)
    dn = pltpu.roll(x, 8, axis=1)
    sw = jnp.where((lane & 15) < 8, up, dn)
    reps = w // LANES
    if reps > 1:
        cos = jnp.concatenate([cos] * reps, axis=1)
        sin = jnp.concatenate([sin] * reps, axis=1)
    return x * cos + sw * sin


def _premix_kernel(*refs, lat, seq_len):
    if lat:
        (x_ref, mod_ref, g_ref, win_ref, wcq_ref, wckv_ref, wkr_ref, bd_ref, gq_ref, wq_ref, gkv_ref, wk_ref,
         wvt_ref, cos_ref, sin_ref, pa_ref, pb_ref, a_ref, b_ref, q_ref, k_ref, vt_ref) = refs
    else:
        (x_ref, mod_ref, g_ref, win_ref, wcq_ref, wckv_ref, wkr_ref, bd_ref, gq_ref, wq_ref, gkv_ref, wk_ref,
         wvt_ref, dc_ref, ds_ref, pa_ref, pb_ref, yc_ref, q_ref, k_ref, vt_ref, ckv_ref, kr_ref) = refs
    gain = g_ref[...] * (1.0 + mod_ref[1:2, :])
    h = _rms_scale(x_ref[...], D_MODEL) * gain + mod_ref[0:1, :]
    hb = h.astype(BF16)
    pa_ref[...] = _dot(hb, win_ref[:, 0:512]).astype(BF16)
    pb_ref[...] = _dot(hb, win_ref[:, 512:1280]).astype(BF16)
    pc = _dot(hb, win_ref[:, 1280:P_MAIN]).astype(BF16)
    ab = _dot(pc, bd_ref[...])
    if lat:
        a_ref[...] = ab[:, 0:GROUP_W]
        b_ref[...] = ab[:, GROUP_W:2 * GROUP_W]
    else:
        abb = ab.astype(BF16)
        for s in range(TM // seq_len):
            rows = slice(s * seq_len, (s + 1) * seq_len)
            yc_ref[rows, :] = (_dot(dc_ref[...], abb[rows, 0:GROUP_W])
                               + _dot(ds_ref[...], abb[rows, GROUP_W:2 * GROUP_W])).astype(BF16)
    cq = _dot(hb, wcq_ref[...])
    ckv = _dot(hb, wckv_ref[...])
    kr = _dot(hb, wkr_ref[...])
    qn = _rms_scale(cq, Q_LORA) * gq_ref[...]
    q4 = _dot(qn.astype(BF16), wq_ref[...])
    ckv_n = _rms_scale(ckv, KV_LORA) * gkv_ref[...]
    cb = ckv_n.astype(BF16)
    kk = _dot(cb, wk_ref[...])
    vt = _dot_nt(wvt_ref[...], cb).astype(BF16)
    if lat:
        vt_ref[...] = vt
        q4 = _rope(q4, cos_ref[...], sin_ref[...])
        kr = _rope(kr, cos_ref[...], sin_ref[...])
    else:
        for s in range(TM // seq_len):
            vt_ref[s] = vt[:, s * seq_len:(s + 1) * seq_len]
        ckv_ref[...] = ckv_n
        kr_ref[...] = kr[:, 0:QK_ROPE]
    q_ref[...] = (q4 * Q_SCALE).astype(BF16)
    for hd in range(MLA_HEADS):
        k_ref[:, hd * HEAD_PAD:(hd + 1) * HEAD_PAD] = (kk[:, hd * HEAD_PAD:(hd + 1) * HEAD_PAD] + kr).astype(BF16)


def _premix(x, mod_l, lw, layer, lat, seq_len, tabs):
    t = x.shape[0]
    nt = t // TM
    nb = t // seq_len
    tpb = max(seq_len // TM, 1)
    row = lambda i: (i, 0)
    if lat:
        mod_map = lambda i: (1 + i // tpb, 0, 0)
        vt_spec = pl.BlockSpec((None, VW, TM), lambda i: (i // tpb, 0, i % tpb))
        tab_specs = [pl.BlockSpec((TM, LANES), lambda i: (i % tpb, 0))] * 2
        fourier_specs = [pl.BlockSpec((TM, GROUP_W), row)] * 2
        fourier_shapes = [jax.ShapeDtypeStruct((t, GROUP_W), F32)] * 2
    else:
        mod_map = lambda i: (0, 0, 0)
        vt_spec = pl.BlockSpec((TM // seq_len, VW, seq_len), lambda i: (i, 0, 0))
        tab_specs = [pl.BlockSpec((seq_len, seq_len), lambda i: (0, 0))] * 2
        fourier_specs = [pl.BlockSpec((TM, GROUP_W), row)]
        fourier_shapes = [jax.ShapeDtypeStruct((t, GROUP_W), BF16)]
    names = ["g_pre_mix", "w_main", "w_cq", "w_ckv", "w_kr", "bd", "g_q", "wq4", "g_kv", "wk4", "wvt"]
    in_specs = [pl.BlockSpec((TM, D_MODEL), row), pl.BlockSpec((None, 6, D_MODEL), mod_map)]
    in_specs += [_layer_spec(lw[n], layer) for n in names] + tab_specs
    args = [x, mod_l] + [lw[n] for n in names] + list(tabs)
    out_specs = [pl.BlockSpec((TM, 512), row), pl.BlockSpec((TM, 768), row)] + fourier_specs + [
        pl.BlockSpec((TM, QW), row),
        pl.BlockSpec((TM, QW), row),
        vt_spec,
    ]
    out_shape = [jax.ShapeDtypeStruct((t, 512), BF16), jax.ShapeDtypeStruct((t, 768), BF16)] + fourier_shapes + [
        jax.ShapeDtypeStruct((t, QW), BF16),
        jax.ShapeDtypeStruct((t, QW), BF16),
        jax.ShapeDtypeStruct((nb, VW, seq_len), BF16),
    ]
    if not lat:
        out_specs += [pl.BlockSpec((TM, KV_LORA), row), pl.BlockSpec((TM, QK_ROPE), row)]
        out_shape += [jax.ShapeDtypeStruct((t, KV_LORA), F32), jax.ShapeDtypeStruct((t, QK_ROPE), F32)]
    return pl.pallas_call(
        functools.partial(_premix_kernel, lat=lat, seq_len=seq_len),
        grid=(nt,),
        in_specs=in_specs,
        out_specs=out_specs,
        out_shape=out_shape,
        compiler_params=_cparams(("arbitrary",)),
        name="premix_lat" if lat else "premix_ctx",
    )(*args)


def _seqmix_chunks(pa_ref, pb_ref, prev_ref, next_ref, ws_ref, sb_ref, cw_ref, cb_ref, o_ref, chunks, seq_len):
    i = pl.program_id(0)
    n_chunks = POST_TM // CHUNK
    grp = lax.broadcasted_iota(jnp.int32, (CHUNK, GROUP_W), 1) // (GROUP_W // A_GROUPS)
    r = lax.broadcasted_iota(jnp.int32, (CHUNK, GROUP_W), 0)

    def z_row(ref, row):
        blk = slice(row // 16 * 16, row // 16 * 16 + 16)
        zz = ref[blk, 2 * GROUP_W:3 * GROUP_W].astype(F32) * ref[blk, 0:GROUP_W].astype(F32)
        return zz[row % 16:row % 16 + 1]

    tile_pos = (i * POST_TM) & (seq_len - 1)
    for c in chunks:
        rows = slice(c * CHUNK, (c + 1) * CHUNK)
        u = pa_ref[rows, 0:GROUP_W].astype(F32)
        v = pa_ref[rows, GROUP_W:2 * GROUP_W]
        m4 = _dot(ws_ref[...], v)
        mixed = m4[0:CHUNK]
        for g in range(1, A_GROUPS):
            mixed = jnp.where(grp == g, m4[g * CHUNK:(g + 1) * CHUNK], mixed)
        o_ref[rows, 0:GROUP_W] = (u * (mixed + sb_ref[...])).astype(BF16)
        z = pb_ref[rows, 2 * GROUP_W:3 * GROUP_W].astype(F32) * pb_ref[rows, 0:GROUP_W].astype(F32)
        if c > 0:
            before = z_row(pb_ref, c * CHUNK - 1) if (c * CHUNK) % seq_len else jnp.zeros((1, GROUP_W), F32)
        else:
            before = jnp.where(tile_pos != 0, z_row(prev_ref, 15), 0.0)
        if c < n_chunks - 1:
            after = z_row(pb_ref, (c + 1) * CHUNK) if ((c + 1) * CHUNK) % seq_len else jnp.zeros((1, GROUP_W), F32)
        else:
            after = jnp.where(((tile_pos + POST_TM) & (seq_len - 1)) != 0, z_row(next_ref, 0), 0.0)
        z_prev = jnp.where(r == 0, before, pltpu.roll(z, 1, axis=0))
        z_next = jnp.where(r == CHUNK - 1, after, pltpu.roll(z, CHUNK - 1, axis=0))
        y = z_prev * cw_ref[0:1, :] + z * cw_ref[1:2, :] + z_next * cw_ref[2:3, :] + cb_ref[...]
        gb = pb_ref[rows, GROUP_W:2 * GROUP_W].astype(F32)
        o_ref[rows, GROUP_W:2 * GROUP_W] = (gb * y).astype(BF16)


def _posdft_kernel(a_ref, b_ref, twc_ref, tws_ref, dc_ref, ds_ref, perm_ref, o_ref, x_scr, y_scr, res_scr, *,
                   levels):
    n = a_ref.shape[0]
    tw0 = 0
    for lv in range(levels):
        size = n >> lv
        half = size // 2
        src_x, src_y = (a_ref, b_ref) if lv == 0 else (x_scr, y_scr)
        for base in range(0, n, size):
            for j0 in range(0, half, DFT_RB):
                top = slice(base + j0, base + j0 + DFT_RB)
                bot = slice(base + half + j0, base + half + j0 + DFT_RB)
                xt, xb, yt, yb = src_x[top, :], src_x[bot, :], src_y[top, :], src_y[bot, :]
                c = twc_ref[tw0 + j0:tw0 + j0 + DFT_RB, :]
                s = tws_ref[tw0 + j0:tw0 + j0 + DFT_RB, :]
                c = jnp.concatenate([c] * (GROUP_W // LANES), axis=1)
                s = jnp.concatenate([s] * (GROUP_W // LANES), axis=1)
                x_scr[top, :] = xt + xb
                y_scr[top, :] = yt + yb
                dx, dy = xt - xb, yt - yb
                x_scr[bot, :] = dx * c - dy * s
                y_scr[bot, :] = dx * s + dy * c
        tw0 += half
    sub = n >> levels
    nres = 1 << levels
    for p in range(nres):
        r = int(format(p, "0%db" % levels)[::-1], 2) if levels else 0
        rows = slice(p * sub, (p + 1) * sub)
        xb = x_scr[rows, :].astype(BF16)
        yb = y_scr[rows, :].astype(BF16)
        res_scr[r] = (_dot(dc_ref[...], xb) + _dot(ds_ref[...], yb)).astype(BF16)
    grp = sub // nres
    for g in range(nres):
        stack = jnp.concatenate([res_scr[r, g * grp:(g + 1) * grp, :] for r in range(nres)], axis=0)
        o_ref[g * sub:(g + 1) * sub, :] = _dot(perm_ref[...], stack).astype(BF16)


def _posdft(a, b, tabs, nb, n):
    twc, tws, dc, ds, perm = tabs
    sub = dc.shape[0]
    nres = n // sub
    levels = nres.bit_length() - 1
    const = lambda i: (0, 0)
    return pl.pallas_call(
        functools.partial(_posdft_kernel, levels=levels),
        grid=(nb,),
        in_specs=[
            pl.BlockSpec((n, GROUP_W), lambda i: (i, 0)),
            pl.BlockSpec((n, GROUP_W), lambda i: (i, 0)),
            pl.BlockSpec(twc.shape, const),
            pl.BlockSpec(tws.shape, const),
            pl.BlockSpec((sub, sub), const),
            pl.BlockSpec((sub, sub), const),
            pl.BlockSpec((sub, sub), const),
        ],
        out_specs=pl.BlockSpec((n, GROUP_W), lambda i: (i, 0)),
        out_shape=jax.ShapeDtypeStruct((nb * n, GROUP_W), BF16),
        scratch_shapes=[pltpu.VMEM((n, GROUP_W), F32), pltpu.VMEM((n, GROUP_W), F32),
                        pltpu.VMEM((nres, sub, GROUP_W), BF16)],
        compiler_params=_cparams(("arbitrary",)),
        name="posdft",
    )(a, b, twc, tws, dc, ds, perm)


def _attn_kernel(*refs, has_cache):
    if has_cache:
        q_ref, k_ref, vt_ref, kc_ref, vct_ref, o_ref, st_scr, pt_scr = refs
    else:
        q_ref, k_ref, vt_ref, o_ref, st_scr, pt_scr = refs
    nk = k_ref.shape[0]
    kc = min(nk, ATT_KC)
    segs = [(k_ref, vt_ref, c * kc, c * kc, kc) for c in range(nk // kc)]
    if has_cache:
        segs.append((kc_ref, vct_ref, 0, nk, kc_ref.shape[0]))
    nsub = q_ref.shape[0] // ATT_SUB
    items = [(s, h) for s in range(nsub) for h in range(MLA_HEADS)]
    outs = [[None] * MLA_HEADS for _ in range(nsub)]
    m_prev = None
    for t in range(len(items) + 1):
        if t < len(items):
            s, h = items[t]
            cols = slice(h * HEAD_PAD, (h + 1) * HEAD_PAD)
            qh = q_ref[s * ATT_SUB:(s + 1) * ATT_SUB, cols]
            m = None
        if t >= 1:
            ps, ph = items[t - 1]
            vrows = slice(ph * V_DIM, (ph + 1) * V_DIM)
            l = jnp.zeros((1, ATT_SUB), F32)
            ot = jnp.zeros((V_DIM, ATT_SUB), F32)
        for ci, (kref, vref, r0, s0, n) in enumerate(segs):
            if t < len(items):
                st_scr[t % 2, s0:s0 + n, :] = _dot_nt(kref[r0:r0 + n, cols], qh)
                for j in range(0, n, ATT_EB):
                    cm = jnp.max(st_scr[t % 2, s0 + j:s0 + j + ATT_EB, :], axis=0, keepdims=True)
                    m = cm if m is None else jnp.maximum(m, cm)
            if t >= 1:
                for j in range(0, n, ATT_EB):
                    p = jnp.exp2(st_scr[(t - 1) % 2, s0 + j:s0 + j + ATT_EB, :] - m_prev)
                    l = l + jnp.sum(p, axis=0, keepdims=True)
                    pt_scr[ci % 2, j:j + ATT_EB, :] = p.astype(BF16)
                ot = ot + _dot(vref[vrows, r0:r0 + n], pt_scr[ci % 2, 0:n, :])
        if t >= 1:
            outs[ps][ph] = ot / l
        if t < len(items):
            m_prev = m
    for s in range(nsub):
        o_ref[s * ATT_SUB:(s + 1) * ATT_SUB, :] = jnp.concatenate(outs[s], axis=0).T.astype(BF16)


def _attention(q, k, vt, nb, seq_len, cache=None, layer=0):
    tq = min(seq_len, TQ)
    nk_total = seq_len + (cache[0].shape[2] if cache is not None else 0)
    q3 = q.reshape(nb, seq_len, QW)
    k3 = k.reshape(nb, seq_len, QW)
    in_specs = [
        pl.BlockSpec((None, tq, QW), lambda b, i: (b, i, 0)),
        pl.BlockSpec((None, seq_len, QW), lambda b, i: (b, 0, 0)),
        pl.BlockSpec((None, VW, seq_len), lambda b, i: (b, 0, 0)),
    ]
    args = [q3, k3, vt]
    if cache is not None:
        kc, vct = cache
        past = kc.shape[2]
        in_specs += [
            pl.BlockSpec((None, None, past, QW), lambda b, i: (b, layer, 0, 0)),
            pl.BlockSpec((None, None, VW, past), lambda b, i: (b, layer, 0, 0)),
        ]
        args += [kc, vct]
    out = pl.pallas_call(
        functools.partial(_attn_kernel, has_cache=cache is not None),
        grid=(nb, seq_len // tq),
        in_specs=in_specs,
        out_specs=pl.BlockSpec((None, tq, VW), lambda b, i: (b, i, 0)),
        out_shape=jax.ShapeDtypeStruct((nb, seq_len, VW), BF16),
        scratch_shapes=[pltpu.VMEM((2, nk_total, ATT_SUB), F32),
                        pltpu.VMEM((2, min(seq_len, ATT_KC), ATT_SUB), BF16)],
        compiler_params=_cparams(("arbitrary", "arbitrary")),
        name="attn_lat" if cache is not None else "attn_ctx",
    )(*args)
    return out.reshape(nb * seq_len, VW)


def _post_kernel(x_ref, mod_ref, pa_ref, pb_ref, prev_ref, next_ref, yc_ref, yd_ref, ws_ref, sb_ref, cw_ref, cb_ref,
                 wo_ref, gpm_ref, gpf_ref, gpo_ref, wgu_ref, wd_ref, o_ref, yab_scr, acc_scr, *, seq_len):
    for hf in range(POST_TM // TM):
        rows = slice(hf * TM, (hf + 1) * TM)
        _seqmix_chunks(pa_ref, pb_ref, prev_ref, next_ref, ws_ref, sb_ref, cw_ref, cb_ref, yab_scr,
                       range(hf * (TM // CHUNK), (hf + 1) * (TM // CHUNK)), seq_len)
        mix = (_dot(yab_scr[rows, :], wo_ref[0:2 * GROUP_W, :])
               + _dot(yc_ref[rows, :], wo_ref[2 * GROUP_W:3 * GROUP_W, :])
               + _dot(yd_ref[rows, :], wo_ref[3 * GROUP_W:4 * GROUP_W, :]))
        x1 = x_ref[rows, :] + _rms_scale(mix, D_MODEL) * (mod_ref[2:3, :] * gpm_ref[...])
        o_ref[rows, :] = x1
        h2 = _rms_scale(x1, D_MODEL) * (gpf_ref[...] * (1.0 + mod_ref[4:5, :])) + mod_ref[3:4, :]
        hb = h2.astype(BF16)
        for c in range(N_FF_CHUNKS):
            g = _dot(hb, wgu_ref[:, c * FF_CHUNK:(c + 1) * FF_CHUNK])
            u = _dot(hb, wgu_ref[:, FF_HIDDEN + c * FF_CHUNK:FF_HIDDEN + (c + 1) * FF_CHUNK])
            act = (_silu(g) * u).astype(BF16)
            d = _dot(act, wd_ref[c * FF_CHUNK:(c + 1) * FF_CHUNK, :])
            if c == 0:
                acc_scr[rows, :] = d
            else:
                acc_scr[rows, :] += d
        o_ref[rows, :] = o_ref[rows, :] + _rms_scale(acc_scr[rows, :], D_MODEL) * (mod_ref[5:6, :] * gpo_ref[...])


def _post(x, mod_l, pa, pb, yc, yd, lw, layer, lat, seq_len):
    t = x.shape[0]
    nt = t // POST_TM
    tpb = max(seq_len // POST_TM, 1)
    mod_map = (lambda i: (1 + i // tpb, 0, 0)) if lat else (lambda i: (0, 0, 0))
    row = lambda i: (i, 0)
    hb = POST_TM // 16
    last = t // 16 - 1
    resident = dict(pipeline_mode=pl.Buffered(1))
    mix_names = ["ws", "sb", "conv_w", "conv_b"]
    return pl.pallas_call(
        functools.partial(_post_kernel, seq_len=seq_len),
        grid=(nt,),
        in_specs=[
            pl.BlockSpec((POST_TM, D_MODEL), row),
            pl.BlockSpec((None, 6, D_MODEL), mod_map),
            pl.BlockSpec((POST_TM, 512), row),
            pl.BlockSpec((POST_TM, 768), row),
            pl.BlockSpec((16, 768), lambda i: (jnp.maximum(i * hb - 1, 0), 0)),
            pl.BlockSpec((16, 768), lambda i: (jnp.minimum((i + 1) * hb, last), 0)),
            pl.BlockSpec((POST_TM, GROUP_W), row),
            pl.BlockSpec((POST_TM, GROUP_W), row),
        ] + [_layer_spec(lw[n], layer) for n in mix_names] + [
            _layer_spec(lw["w_out"], layer, **resident),
            _layer_spec(lw["g_post_mix"], layer),
            _layer_spec(lw["g_pre_ffn"], layer),
            _layer_spec(lw["g_post_ffn"], layer),
            _layer_spec(lw["w_gate_up"], layer, **resident),
            _layer_spec(lw["w_down"], layer, **resident),
        ],
        out_specs=pl.BlockSpec((POST_TM, D_MODEL), row),
        out_shape=jax.ShapeDtypeStruct((t, D_MODEL), F32),
        scratch_shapes=[pltpu.VMEM((POST_TM, 2 * GROUP_W), BF16), pltpu.VMEM((POST_TM, D_MODEL), F32)],
        compiler_params=_cparams(("arbitrary",)),
        name="post_lat" if lat else "post_ctx",
    )(x, mod_l, pa, pb, pb, pb, yc, yd, *[lw[n] for n in mix_names], lw["w_out"], lw["g_post_mix"],
      lw["g_pre_ffn"], lw["g_post_ffn"], lw["w_gate_up"], lw["w_down"])


def _dft_tables(n, seq_len):
    scale = seq_len ** -0.5
    k = jnp.arange(n, dtype=jnp.int32)
    ang = ((k[:, None] * k[None, :]) % n).astype(F32) * (2.0 * math.pi / n)
    return (jnp.cos(ang) * scale).astype(BF16), (-jnp.sin(ang) * scale).astype(BF16)


def _twiddle_tables(n, sub):
    angs = []
    size = n
    while size > sub:
        angs.append(jnp.arange(size // 2, dtype=F32) * (2.0 * math.pi / size))
        size //= 2
    ang = jnp.concatenate(angs)[:, None]
    return jnp.broadcast_to(jnp.cos(ang), (ang.shape[0], LANES)), jnp.broadcast_to(jnp.sin(ang), (ang.shape[0], LANES))


def _interleave_perm(sub, nres):
    row = jnp.arange(sub, dtype=jnp.int32)
    src = (row % nres) * (sub // nres) + row // nres
    return (src[:, None] == row[None, :]).astype(BF16)


def _channel_dft():
    c = jnp.arange(C_CH, dtype=jnp.int32)
    ang = ((c[:, None] * c[None, :]) % C_CH).astype(F32) * (2.0 * math.pi / C_CH)
    eye = jnp.eye(C_GROUPS, dtype=F32)
    bc = jnp.kron(eye, jnp.cos(ang) * C_CH ** -0.5)
    bs = jnp.kron(eye, jnp.sin(ang) * C_CH ** -0.5)
    return jnp.concatenate([bc, bs], axis=1).astype(BF16)


def _rope_tables(n):
    pos = jnp.arange(n, dtype=jnp.int32)
    row = (pos // GRID_W).astype(F32)
    col = (pos % GRID_W).astype(F32)
    inv = ROPE_BASE ** (-jnp.arange(0, AXIS_ROPE, 2, dtype=F32) / AXIS_ROPE)
    half = AXIS_ROPE // 2
    ang = jnp.concatenate([row[:, None] * inv, row[:, None] * inv, col[:, None] * inv, col[:, None] * inv], axis=1)
    sign = jnp.tile(jnp.concatenate([-jnp.ones((half,), F32), jnp.ones((half,), F32)]), 2)
    cos = jnp.concatenate([jnp.cos(ang), jnp.ones((n, LANES - QK_ROPE), F32)], axis=1)
    sin = jnp.concatenate([jnp.sin(ang) * sign, jnp.zeros((n, LANES - QK_ROPE), F32)], axis=1)
    return cos, sin


def _layout_weights(g_pre_mix, g_post_mix, g_pre_ffn, g_post_ffn, w_in, spat_w, spat_b, conv_w, conv_b,
                    g_q_lora, w_uq, g_kv_lora, w_ukv, w_out, w_gate_up, w_down):
    d = DEPTH
    zeros = lambda *s: jnp.zeros(s, F32)
    c_kv0 = P_MAIN + Q_LORA
    c_kr0 = c_kv0 + KV_LORA
    lane_pad = lambda w, width: jnp.pad(w, ((0, 0), (0, 0), (0, width - w.shape[2]))).astype(BF16)
    uq = w_uq.reshape(d, Q_LORA, MLA_HEADS, QK_NOPE + QK_ROPE)
    wq4 = jnp.concatenate([uq[..., QK_NOPE:], uq[..., :QK_NOPE],
                           zeros(d, Q_LORA, MLA_HEADS, HEAD_PAD - QK_NOPE - QK_ROPE)], axis=3)
    wq4 = wq4.reshape(d, Q_LORA, QW)
    wq4 = jnp.concatenate([wq4, zeros(d, Q_LORA_PAD - Q_LORA, QW)], axis=1).astype(BF16)
    ukv = w_ukv.reshape(d, KV_LORA, MLA_HEADS, QK_NOPE + V_DIM)
    wk4 = jnp.concatenate([zeros(d, KV_LORA, MLA_HEADS, QK_ROPE), ukv[..., :QK_NOPE],
                           zeros(d, KV_LORA, MLA_HEADS, HEAD_PAD - QK_NOPE - QK_ROPE)], axis=3)
    wk4 = wk4.reshape(d, KV_LORA, QW).astype(BF16)
    wvt = ukv[..., QK_NOPE:].reshape(d, KV_LORA, VW).transpose(0, 2, 1).astype(BF16)
    g_q = jnp.concatenate([g_q_lora, zeros(d, Q_LORA_PAD - Q_LORA)], axis=1)
    return dict(
        g_pre_mix=g_pre_mix.reshape(d, 1, D_MODEL), g_post_mix=g_post_mix.reshape(d, 1, D_MODEL),
        g_pre_ffn=g_pre_ffn.reshape(d, 1, D_MODEL), g_post_ffn=g_post_ffn.reshape(d, 1, D_MODEL),
        w_main=w_in[:, :, :P_MAIN].astype(BF16),
        w_cq=lane_pad(w_in[:, :, P_MAIN:c_kv0], Q_LORA_PAD),
        w_ckv=w_in[:, :, c_kv0:c_kr0].astype(BF16),
        w_kr=lane_pad(w_in[:, :, c_kr0:], LANES),
        wq4=wq4, wk4=wk4, wvt=wvt,
        g_q=g_q.reshape(d, 1, Q_LORA_PAD), g_kv=g_kv_lora.reshape(d, 1, KV_LORA),
        ws=spat_w.reshape(d, A_GROUPS * CHUNK, CHUNK).astype(BF16),
        sb=jnp.repeat(spat_b.transpose(0, 2, 1), GROUP_W // A_GROUPS, axis=2),
        conv_w=conv_w, conv_b=conv_b.reshape(d, 1, GROUP_W),
        w_out=w_out.astype(BF16), w_gate_up=w_gate_up.astype(BF16), w_down=w_down.astype(BF16),
        bd=jnp.broadcast_to(_channel_dft(), (d, GROUP_W, 2 * GROUP_W)),
    )


def _trunk_layer(x, mod_l, lw, layer, lat, nb, seq_len, premix_tabs, dft_tabs, cache):
    outs = _premix(x, mod_l, lw, layer, lat, seq_len, premix_tabs)
    if lat:
        pa, pb, a, b, q, k, vt = outs
        yc = _posdft(a, b, dft_tabs, nb, seq_len)
        states = ()
    else:
        pa, pb, yc, q, k, vt = outs[:6]
        states = outs[6:]
    yd = _attention(q, k, vt, nb, seq_len, cache, layer)
    x = _post(x, mod_l, pa, pb, yc, yd, lw, layer, lat, seq_len)
    return x, states


def kernel(x_prompt, x_sample, cache_ckv, cache_krope, c, c_ctx, w_ada, b_ada, g_pre_mix, g_post_mix, g_pre_ffn,
           g_post_ffn, w_in, spat_w, spat_b, conv_w, conv_b, g_q_lora, w_uq, g_kv_lora, w_ukv, w_out, w_gate_up,
           w_down):
    batch, seq, _ = x_prompt.shape
    dec_batch, dec_seq, _ = x_sample.shape
    assert (batch * seq) % POST_TM == 0 and TM % seq == 0 and dec_seq % POST_TM == 0 and dec_batch + 1 <= 8

    stacked = _layout_weights(g_pre_mix, g_post_mix, g_pre_ffn, g_post_ffn, w_in, spat_w, spat_b, conv_w, conv_b,
                              g_q_lora, w_uq, g_kv_lora, w_ukv, w_out, w_gate_up, w_down)
    dft_ctx = _dft_tables(seq, seq)
    dft_sub = min(dec_seq, DFT_SUB)
    dft_lat = (_twiddle_tables(dec_seq, dft_sub) + _dft_tables(dft_sub, dec_seq)
               + (_interleave_perm(dft_sub, dec_seq // dft_sub),))
    rope_tabs = _rope_tables(dec_seq)

    cond8 = jnp.concatenate([c_ctx[None, :], c, jnp.zeros((8 - 1 - dec_batch, D_MODEL), F32)], axis=0)
    mod = _modulation(cond8, w_ada, b_ada).reshape(DEPTH, 8, 6, D_MODEL)

    cache_kr_pad = jnp.pad(cache_krope, ((0, 0), (0, 0), (0, 0), (0, LANES - QK_ROPE)))
    cache = _cache_kv(cache_ckv, cache_kr_pad, stacked["wk4"], stacked["wvt"])

    y_ctx = x_prompt.reshape(batch * seq, D_MODEL)
    y_lat = x_sample.reshape(dec_batch * dec_seq, D_MODEL)
    ckv_list, krope_list = [], []
    for l in range(DEPTH):
        y_ctx, (ckv_l, kr_l) = _trunk_layer(y_ctx, mod[l], stacked, l, False, batch, seq, dft_ctx, None, None)
        ckv_list.append(ckv_l.reshape(batch, seq, KV_LORA))
        krope_list.append(kr_l.reshape(batch, seq, QK_ROPE))
        y_lat, _ = _trunk_layer(y_lat, mod[l], stacked, l, True, dec_batch, dec_seq, rope_tabs, dft_lat, cache)
    state_ckv = jnp.stack(ckv_list, axis=1)
    state_krope = jnp.stack(krope_list, axis=1)
    return (y_ctx.reshape(batch, seq, D_MODEL), y_lat.reshape(dec_batch, dec_seq, D_MODEL), state_ckv, state_krope)
```

```python
import functools
import math

import jax
import jax.numpy as jnp
from jax import lax
from jax.experimental import pallas as pl
from jax.experimental.pallas import tpu as pltpu

F32 = jnp.float32
BF16 = jnp.bfloat16

D_MODEL = 1024
DEPTH = 4
GRID_W = 64
GROUP_W = 256
A_GROUPS = 4
CHUNK = 128
C_GROUPS = 4
C_CH = 64
MLA_HEADS = 4
QK_NOPE = 64
QK_ROPE = 32
V_DIM = 64
Q_LORA = 192
KV_LORA = 128
ROPE_BASE = 10000.0
AXIS_ROPE = QK_ROPE // 2
FF_HIDDEN = 2816
EPS = 1e-6

LANES = 128
HEAD_PAD = 128
Q_LORA_PAD = 256
P_MAIN = 512 + 768 + 256
QW = MLA_HEADS * HEAD_PAD
VW = MLA_HEADS * V_DIM
FF_CHUNK = 256
N_FF_CHUNKS = FF_HIDDEN // FF_CHUNK
TM = 512
POST_TM = 1024
TQ = 512
ATT_SUB = 256
ATT_KC = 2048
ATT_EB = 128
DFT_SUB = 512
DFT_RB = 32
VMEM_LIMIT = 56 * 1024 * 1024
Q_SCALE = (QK_NOPE + QK_ROPE) ** -0.5 * math.log2(math.e)


def _cparams(sem, **flags):
    return pltpu.CompilerParams(dimension_semantics=sem, vmem_limit_bytes=VMEM_LIMIT, flags=flags or None)


def _layer_spec(w, layer, **kw):
    tail = w.shape[1:]
    nz = (0,) * len(tail)
    return pl.BlockSpec((None,) + tail, lambda *_: (layer,) + nz, **kw)


def _rms_scale(x, n):
    ms = jnp.sum(x * x, axis=-1, keepdims=True) * (1.0 / n)
    return x * lax.rsqrt(ms + EPS)


def _silu(x):
    return x * jax.nn.sigmoid(x)


def _dot(a, b):
    return jnp.dot(a, b, preferred_element_type=F32)


def _dot_nt(a, b):
    return lax.dot_general(a, b, (((1,), (1,)), ((), ())), preferred_element_type=F32)


def _mod_kernel(cond_ref, w_ref, b_ref, o_ref):
    s = _silu(cond_ref[...])
    o_ref[...] = _dot(s.astype(BF16), w_ref[...].astype(BF16)) + b_ref[...]


def _modulation(cond8, w_ada, b_ada):
    nt = 6 * D_MODEL // 1024
    return pl.pallas_call(
        _mod_kernel,
        grid=(DEPTH, nt),
        in_specs=[
            pl.BlockSpec((8, D_MODEL), lambda l, j: (0, 0)),
            pl.BlockSpec((None, D_MODEL, 1024), lambda l, j: (l, 0, j)),
            pl.BlockSpec((None, 1, 1024), lambda l, j: (l, 0, j)),
        ],
        out_specs=pl.BlockSpec((None, 8, 1024), lambda l, j: (l, 0, j)),
        out_shape=jax.ShapeDtypeStruct((DEPTH, 8, 6 * D_MODEL), F32),
        compiler_params=_cparams(("arbitrary", "arbitrary")),
        name="modulation",
    )(cond8, w_ada, b_ada.reshape(DEPTH, 1, 6 * D_MODEL))


def _cachekv_kernel(ckv_ref, kr_ref, wk_ref, wvt_ref, k_ref, vt_ref):
    c = ckv_ref[...].astype(BF16)
    kk = _dot(c, wk_ref[...])
    kr = kr_ref[...]
    for h in range(MLA_HEADS):
        k_ref[:, h * HEAD_PAD:(h + 1) * HEAD_PAD] = (kk[:, h * HEAD_PAD:(h + 1) * HEAD_PAD] + kr).astype(BF16)
    vt_ref[...] = _dot_nt(wvt_ref[...], c).astype(BF16)


def _cache_kv(cache_ckv, cache_kr_pad, wk4, wvt):
    nb, _, past, _ = cache_ckv.shape
    return pl.pallas_call(
        _cachekv_kernel,
        grid=(nb, DEPTH),
        in_specs=[
            pl.BlockSpec((None, None, past, KV_LORA), lambda b, l: (b, l, 0, 0)),
            pl.BlockSpec((None, None, past, LANES), lambda b, l: (b, l, 0, 0)),
            pl.BlockSpec((None, KV_LORA, QW), lambda b, l: (l, 0, 0)),
            pl.BlockSpec((None, VW, KV_LORA), lambda b, l: (l, 0, 0)),
        ],
        out_specs=[
            pl.BlockSpec((None, None, past, QW), lambda b, l: (b, l, 0, 0)),
            pl.BlockSpec((None, None, VW, past), lambda b, l: (b, l, 0, 0)),
        ],
        out_shape=[
            jax.ShapeDtypeStruct((nb, DEPTH, past, QW), BF16),
            jax.ShapeDtypeStruct((nb, DEPTH, VW, past), BF16),
        ],
        compiler_params=_cparams(("arbitrary", "arbitrary")),
        name="cache_kv",
    )(cache_ckv, cache_kr_pad, wk4, wvt)


def _rope(x, cos, sin):
    w = x.shape[1]
    lane = lax.broadcasted_iota(jnp.int32, x.shape, 1)
    up = pltpu.roll(x, w - 8, axis=1)
    dn = pltpu.roll(x, 8, axis=1)
    sw = jnp.where((lane & 15) < 8, up, dn)
    reps = w // LANES
    if reps > 1:
        cos = jnp.concatenate([cos] * reps, axis=1)
        sin = jnp.concatenate([sin] * reps, axis=1)
    return x * cos + sw * sin


def _premix_kernel(*refs, lat, seq_len):
    if lat:
        (x_ref, mod_ref, g_ref, win_ref, wcq_ref, wckv_ref, wkr_ref, bd_ref, gq_ref, wq_ref, gkv_ref, wk_ref,
         wvt_ref, cos_ref, sin_ref, pa_ref, pb_ref, a_ref, b_ref, q_ref, k_ref, vt_ref) = refs
    else:
        (x_ref, mod_ref, g_ref, win_ref, wcq_ref, wckv_ref, wkr_ref, bd_ref, gq_ref, wq_ref, gkv_ref, wk_ref,
         wvt_ref, dc_ref, ds_ref, pa_ref, pb_ref, yc_ref, q_ref, k_ref, vt_ref, ckv_ref, kr_ref) = refs
    gain = g_ref[...] * (1.0 + mod_ref[1:2, :])
    h = _rms_scale(x_ref[...], D_MODEL) * gain + mod_ref[0:1, :]
    hb = h.astype(BF16)
    pa_ref[...] = _dot(hb, win_ref[:, 0:512]).astype(BF16)
    pb_ref[...] = _dot(hb, win_ref[:, 512:1280]).astype(BF16)
    pc = _dot(hb, win_ref[:, 1280:P_MAIN]).astype(BF16)
    ab = _dot(pc, bd_ref[...])
    if lat:
        a_ref[...] = ab[:, 0:GROUP_W]
        b_ref[...] = ab[:, GROUP_W:2 * GROUP_W]
    else:
        abb = ab.astype(BF16)
        for s in range(TM // seq_len):
            rows = slice(s * seq_len, (s + 1) * seq_len)
            yc_ref[rows, :] = (_dot(dc_ref[...], abb[rows, 0:GROUP_W])
                               + _dot(ds_ref[...], abb[rows, GROUP_W:2 * GROUP_W])).astype(BF16)
    cq = _dot(hb, wcq_ref[...])
    ckv = _dot(hb, wckv_ref[...])
    kr = _dot(hb, wkr_ref[...])
    qn = _rms_scale(cq, Q_LORA) * gq_ref[...]
    q4 = _dot(qn.astype(BF16), wq_ref[...])
    ckv_n = _rms_scale(ckv, KV_LORA) * gkv_ref[...]
    cb = ckv_n.astype(BF16)
    kk = _dot(cb, wk_ref[...])
    vt = _dot_nt(wvt_ref[...], cb).astype(BF16)
    if lat:
        vt_ref[...] = vt
        q4 = _rope(q4, cos_ref[...], sin_ref[...])
        kr = _rope(kr, cos_ref[...], sin_ref[...])
    else:
        for s in range(TM // seq_len):
            vt_ref[s] = vt[:, s * seq_len:(s + 1) * seq_len]
        ckv_ref[...] = ckv_n
        kr_ref[...] = kr[:, 0:QK_ROPE]
    q_ref[...] = (q4 * Q_SCALE).astype(BF16)
    for hd in range(MLA_HEADS):
        k_ref[:, hd * HEAD_PAD:(hd + 1) * HEAD_PAD] = (kk[:, hd * HEAD_PAD:(hd + 1) * HEAD_PAD] + kr).astype(BF16)


def _premix(x, mod_l, lw, layer, lat, seq_len, tabs):
    t = x.shape[0]
    nt = t // TM
    nb = t // seq_len
    tpb = max(seq_len // TM, 1)
    row = lambda i: (i, 0)
    if lat:
        mod_map = lambda i: (1 + i // tpb, 0, 0)
        vt_spec = pl.BlockSpec((None, VW, TM), lambda i: (i // tpb, 0, i % tpb))
        tab_specs = [pl.BlockSpec((TM, LANES), lambda i: (i % tpb, 0))] * 2
        fourier_specs = [pl.BlockSpec((TM, GROUP_W), row)] * 2
        fourier_shapes = [jax.ShapeDtypeStruct((t, GROUP_W), F32)] * 2
    else:
        mod_map = lambda i: (0, 0, 0)
        vt_spec = pl.BlockSpec((TM // seq_len, VW, seq_len), lambda i: (i, 0, 0))
        tab_specs = [pl.BlockSpec((seq_len, seq_len), lambda i: (0, 0))] * 2
        fourier_specs = [pl.BlockSpec((TM, GROUP_W), row)]
        fourier_shapes = [jax.ShapeDtypeStruct((t, GROUP_W), BF16)]
    names = ["g_pre_mix", "w_main", "w_cq", "w_ckv", "w_kr", "bd", "g_q", "wq4", "g_kv", "wk4", "wvt"]
    in_specs = [pl.BlockSpec((TM, D_MODEL), row), pl.BlockSpec((None, 6, D_MODEL), mod_map)]
    in_specs += [_layer_spec(lw[n], layer) for n in names] + tab_specs
    args = [x, mod_l] + [lw[n] for n in names] + list(tabs)
    out_specs = [pl.BlockSpec((TM, 512), row), pl.BlockSpec((TM, 768), row)] + fourier_specs + [
        pl.BlockSpec((TM, QW), row),
        pl.BlockSpec((TM, QW), row),
        vt_spec,
    ]
    out_shape = [jax.ShapeDtypeStruct((t, 512), BF16), jax.ShapeDtypeStruct((t, 768), BF16)] + fourier_shapes + [
        jax.ShapeDtypeStruct((t, QW), BF16),
        jax.ShapeDtypeStruct((t, QW), BF16),
        jax.ShapeDtypeStruct((nb, VW, seq_len), BF16),
    ]
    if not lat:
        out_specs += [pl.BlockSpec((TM, KV_LORA), row), pl.BlockSpec((TM, QK_ROPE), row)]
        out_shape += [jax.ShapeDtypeStruct((t, KV_LORA), F32), jax.ShapeDtypeStruct((t, QK_ROPE), F32)]
    return pl.pallas_call(
        functools.partial(_premix_kernel, lat=lat, seq_len=seq_len),
        grid=(nt,),
        in_specs=in_specs,
        out_specs=out_specs,
        out_shape=out_shape,
        compiler_params=_cparams(("arbitrary",)),
        name="premix_lat" if lat else "premix_ctx",
    )(*args)


def _seqmix_chunks(pa_ref, pb_ref, prev_ref, next_ref, ws_ref, sb_ref, cw_ref, cb_ref, o_ref, chunks, seq_len):
    i = pl.program_id(0)
    n_chunks = POST_TM // CHUNK
    grp = lax.broadcasted_iota(jnp.int32, (CHUNK, GROUP_W), 1) // (GROUP_W // A_GROUPS)
    r = lax.broadcasted_iota(jnp.int32, (CHUNK, GROUP_W), 0)

    def z_row(ref, row):
        blk = slice(row // 16 * 16, row // 16 * 16 + 16)
        zz = ref[blk, 2 * GROUP_W:3 * GROUP_W].astype(F32) * ref[blk, 0:GROUP_W].astype(F32)
        return zz[row % 16:row % 16 + 1]

    tile_pos = (i * POST_TM) & (seq_len - 1)
    for c in chunks:
        rows = slice(c * CHUNK, (c + 1) * CHUNK)
        u = pa_ref[rows, 0:GROUP_W].astype(F32)
        v = pa_ref[rows, GROUP_W:2 * GROUP_W]
        m4 = _dot(ws_ref[...], v)
        mixed = m4[0:CHUNK]
        for g in range(1, A_GROUPS):
            mixed = jnp.where(grp == g, m4[g * CHUNK:(g + 1) * CHUNK], mixed)
        o_ref[rows, 0:GROUP_W] = (u * (mixed + sb_ref[...])).astype(BF16)
        z = pb_ref[rows, 2 * GROUP_W:3 * GROUP_W].astype(F32) * pb_ref[rows, 0:GROUP_W].astype(F32)
        if c > 0:
            before = z_row(pb_ref, c * CHUNK - 1) if (c * CHUNK) % seq_len else jnp.zeros((1, GROUP_W), F32)
        else:
            before = jnp.where(tile_pos != 0, z_row(prev_ref, 15), 0.0)
        if c < n_chunks - 1:
            after = z_row(pb_ref, (c + 1) * CHUNK) if ((c + 1) * CHUNK) % seq_len else jnp.zeros((1, GROUP_W), F32)
        else:
            after = jnp.where(((tile_pos + POST_TM) & (seq_len - 1)) != 0, z_row(next_ref, 0), 0.0)
        z_prev = jnp.where(r == 0, before, pltpu.roll(z, 1, axis=0))
        z_next = jnp.where(r == CHUNK - 1, after, pltpu.roll(z, CHUNK - 1, axis=0))
        y = z_prev * cw_ref[0:1, :] + z * cw_ref[1:2, :] + z_next * cw_ref[2:3, :] + cb_ref[...]
        gb = pb_ref[rows, GROUP_W:2 * GROUP_W].astype(F32)
        o_ref[rows, GROUP_W:2 * GROUP_W] = (gb * y).astype(BF16)


def _posdft_kernel(a_ref, b_ref, twc_ref, tws_ref, dc_ref, ds_ref, perm_ref, o_ref, x_scr, y_scr, res_scr, *,
                   levels):
    n = a_ref.shape[0]
    tw0 = 0
    for lv in range(levels):
        size = n >> lv
        half = size // 2
        src_x, src_y = (a_ref, b_ref) if lv == 0 else (x_scr, y_scr)
        for base in range(0, n, size):
            for j0 in range(0, half, DFT_RB):
                top = slice(base + j0, base + j0 + DFT_RB)
                bot = slice(base + half + j0, base + half + j0 + DFT_RB)
                xt, xb, yt, yb = src_x[top, :], src_x[bot, :], src_y[top, :], src_y[bot, :]
                c = twc_ref[tw0 + j0:tw0 + j0 + DFT_RB, :]
                s = tws_ref[tw0 + j0:tw0 + j0 + DFT_RB, :]
                c = jnp.concatenate([c] * (GROUP_W // LANES), axis=1)
                s = jnp.concatenate([s] * (GROUP_W // LANES), axis=1)
                x_scr[top, :] = xt + xb
                y_scr[top, :] = yt + yb
                dx, dy = xt - xb, yt - yb
                x_scr[bot, :] = dx * c - dy * s
                y_scr[bot, :] = dx * s + dy * c
        tw0 += half
    sub = n >> levels
    nres = 1 << levels
    for p in range(nres):
        r = int(format(p, "0%db" % levels)[::-1], 2) if levels else 0
        rows = slice(p * sub, (p + 1) * sub)
        xb = x_scr[rows, :].astype(BF16)
        yb = y_scr[rows, :].astype(BF16)
        res_scr[r] = (_dot(dc_ref[...], xb) + _dot(ds_ref[...], yb)).astype(BF16)
    grp = sub // nres
    for g in range(nres):
        stack = jnp.concatenate([res_scr[r, g * grp:(g + 1) * grp, :] for r in range(nres)], axis=0)
        o_ref[g * sub:(g + 1) * sub, :] = _dot(perm_ref[...], stack).astype(BF16)


def _posdft(a, b, tabs, nb, n):
    twc, tws, dc, ds, perm = tabs
    sub = dc.shape[0]
    nres = n // sub
    levels = nres.bit_length() - 1
    const = lambda i: (0, 0)
    return pl.pallas_call(
        functools.partial(_posdft_kernel, levels=levels),
        grid=(nb,),
        in_specs=[
            pl.BlockSpec((n, GROUP_W), lambda i: (i, 0)),
            pl.BlockSpec((n, GROUP_W), lambda i: (i, 0)),
            pl.BlockSpec(twc.shape, const),
            pl.BlockSpec(tws.shape, const),
            pl.BlockSpec((sub, sub), const),
            pl.BlockSpec((sub, sub), const),
            pl.BlockSpec((sub, sub), const),
        ],
        out_specs=pl.BlockSpec((n, GROUP_W), lambda i: (i, 0)),
        out_shape=jax.ShapeDtypeStruct((nb * n, GROUP_W), BF16),
        scratch_shapes=[pltpu.VMEM((n, GROUP_W), F32), pltpu.VMEM((n, GROUP_W), F32),
                        pltpu.VMEM((nres, sub, GROUP_W), BF16)],
        compiler_params=_cparams(("arbitrary",)),
        name="posdft",
    )(a, b, twc, tws, dc, ds, perm)


def _attn_kernel(*refs, has_cache):
    if has_cache:
        q_ref, k_ref, vt_ref, kc_ref, vct_ref, o_ref, st_scr, pt_scr = refs
    else:
        q_ref, k_ref, vt_ref, o_ref, st_scr, pt_scr = refs
    nk = k_ref.shape[0]
    kc = min(nk, ATT_KC)
    segs = [(k_ref, vt_ref, c * kc, c * kc, kc) for c in range(nk // kc)]
    if has_cache:
        segs.append((kc_ref, vct_ref, 0, nk, kc_ref.shape[0]))
    nsub = q_ref.shape[0] // ATT_SUB
    items = [(s, h) for s in range(nsub) for h in range(MLA_HEADS)]
    outs = [[None] * MLA_HEADS for _ in range(nsub)]
    m_prev = None
    for t in range(len(items) + 1):
        if t < len(items):
            s, h = items[t]
            cols = slice(h * HEAD_PAD, (h + 1) * HEAD_PAD)
            qh = q_ref[s * ATT_SUB:(s + 1) * ATT_SUB, cols]
            m = None
        if t >= 1:
            ps, ph = items[t - 1]
            vrows = slice(ph * V_DIM, (ph + 1) * V_DIM)
            l = jnp.zeros((1, ATT_SUB), F32)
            ot = jnp.zeros((V_DIM, ATT_SUB), F32)
        for ci, (kref, vref, r0, s0, n) in enumerate(segs):
            if t < len(items):
                st_scr[t % 2, s0:s0 + n, :] = _dot_nt(kref[r0:r0 + n, cols], qh)
                for j in range(0, n, ATT_EB):
                    cm = jnp.max(st_scr[t % 2, s0 + j:s0 + j + ATT_EB, :], axis=0, keepdims=True)
                    m = cm if m is None else jnp.maximum(m, cm)
            if t >= 1:
                for j in range(0, n, ATT_EB):
                    p = jnp.exp2(st_scr[(t - 1) % 2, s0 + j:s0 + j + ATT_EB, :] - m_prev)
                    l = l + jnp.sum(p, axis=0, keepdims=True)
                    pt_scr[ci % 2, j:j + ATT_EB, :] = p.astype(BF16)
                ot = ot + _dot(vref[vrows, r0:r0 + n], pt_scr[ci % 2, 0:n, :])
        if t >= 1:
            outs[ps][ph] = ot / l
        if t < len(items):
            m_prev = m
    for s in range(nsub):
        o_ref[s * ATT_SUB:(s + 1) * ATT_SUB, :] = jnp.concatenate(outs[s], axis=0).T.astype(BF16)


def _attention(q, k, vt, nb, seq_len, cache=None, layer=0):
    tq = min(seq_len, TQ)
    nk_total = seq_len + (cache[0].shape[2] if cache is not None else 0)
    q3 = q.reshape(nb, seq_len, QW)
    k3 = k.reshape(nb, seq_len, QW)
    in_specs = [
        pl.BlockSpec((None, tq, QW), lambda b, i: (b, i, 0)),
        pl.BlockSpec((None, seq_len, QW), lambda b, i: (b, 0, 0)),
        pl.BlockSpec((None, VW, seq_len), lambda b, i: (b, 0, 0)),
    ]
    args = [q3, k3, vt]
    if cache is not None:
        kc, vct = cache
        past = kc.shape[2]
        in_specs += [
            pl.BlockSpec((None, None, past, QW), lambda b, i: (b, layer, 0, 0)),
            pl.BlockSpec((None, None, VW, past), lambda b, i: (b, layer, 0, 0)),
        ]
        args += [kc, vct]
    out = pl.pallas_call(
        functools.partial(_attn_kernel, has_cache=cache is not None),
        grid=(nb, seq_len // tq),
        in_specs=in_specs,
        out_specs=pl.BlockSpec((None, tq, VW), lambda b, i: (b, i, 0)),
        out_shape=jax.ShapeDtypeStruct((nb, seq_len, VW), BF16),
        scratch_shapes=[pltpu.VMEM((2, nk_total, ATT_SUB), F32),
                        pltpu.VMEM((2, min(seq_len, ATT_KC), ATT_SUB), BF16)],
        compiler_params=_cparams(("arbitrary", "arbitrary")),
        name="attn_lat" if cache is not None else "attn_ctx",
    )(*args)
    return out.reshape(nb * seq_len, VW)


def _post_kernel(x_ref, mod_ref, pa_ref, pb_ref, prev_ref, next_ref, yc_ref, yd_ref, ws_ref, sb_ref, cw_ref, cb_ref,
                 wo_ref, gpm_ref, gpf_ref, gpo_ref, wgu_ref, wd_ref, o_ref, yab_scr, acc_scr, *, seq_len):
    for hf in range(POST_TM // TM):
        rows = slice(hf * TM, (hf + 1) * TM)
        _seqmix_chunks(pa_ref, pb_ref, prev_ref, next_ref, ws_ref, sb_ref, cw_ref, cb_ref, yab_scr,
                       range(hf * (TM // CHUNK), (hf + 1) * (TM // CHUNK)), seq_len)
        mix = (_dot(yab_scr[rows, :], wo_ref[0:2 * GROUP_W, :])
               + _dot(yc_ref[rows, :], wo_ref[2 * GROUP_W:3 * GROUP_W, :])
               + _dot(yd_ref[rows, :], wo_ref[3 * GROUP_W:4 * GROUP_W, :]))
        x1 = x_ref[rows, :] + _rms_scale(mix, D_MODEL) * (mod_ref[2:3, :] * gpm_ref[...])
        o_ref[rows, :] = x1
        h2 = _rms_scale(x1, D_MODEL) * (gpf_ref[...] * (1.0 + mod_ref[4:5, :])) + mod_ref[3:4, :]
        hb = h2.astype(BF16)
        for c in range(N_FF_CHUNKS):
            g = _dot(hb, wgu_ref[:, c * FF_CHUNK:(c + 1) * FF_CHUNK])
            u = _dot(hb, wgu_ref[:, FF_HIDDEN + c * FF_CHUNK:FF_HIDDEN + (c + 1) * FF_CHUNK])
            act = (_silu(g) * u).astype(BF16)
            d = _dot(act, wd_ref[c * FF_CHUNK:(c + 1) * FF_CHUNK, :])
            if c == 0:
                acc_scr[rows, :] = d
            else:
                acc_scr[rows, :] += d
        o_ref[rows, :] = o_ref[rows, :] + _rms_scale(acc_scr[rows, :], D_MODEL) * (mod_ref[5:6, :] * gpo_ref[...])


def _post(x, mod_l, pa, pb, yc, yd, lw, layer, lat, seq_len):
    t = x.shape[0]
    nt = t // POST_TM
    tpb = max(seq_len // POST_TM, 1)
    mod_map = (lambda i: (1 + i // tpb, 0, 0)) if lat else (lambda i: (0, 0, 0))
    row = lambda i: (i, 0)
    hb = POST_TM // 16
    last = t // 16 - 1
    resident = dict(pipeline_mode=pl.Buffered(1))
    mix_names = ["ws", "sb", "conv_w", "conv_b"]
    return pl.pallas_call(
        functools.partial(_post_kernel, seq_len=seq_len),
        grid=(nt,),
        in_specs=[
            pl.BlockSpec((POST_TM, D_MODEL), row),
            pl.BlockSpec((None, 6, D_MODEL), mod_map),
            pl.BlockSpec((POST_TM, 512), row),
            pl.BlockSpec((POST_TM, 768), row),
            pl.BlockSpec((16, 768), lambda i: (jnp.maximum(i * hb - 1, 0), 0)),
            pl.BlockSpec((16, 768), lambda i: (jnp.minimum((i + 1) * hb, last), 0)),
            pl.BlockSpec((POST_TM, GROUP_W), row),
            pl.BlockSpec((POST_TM, GROUP_W), row),
        ] + [_layer_spec(lw[n], layer) for n in mix_names] + [
            _layer_spec(lw["w_out"], layer, **resident),
            _layer_spec(lw["g_post_mix"], layer),
            _layer_spec(lw["g_pre_ffn"], layer),
            _layer_spec(lw["g_post_ffn"], layer),
            _layer_spec(lw["w_gate_up"], layer, **resident),
            _layer_spec(lw["w_down"], layer, **resident),
        ],
        out_specs=pl.BlockSpec((POST_TM, D_MODEL), row),
        out_shape=jax.ShapeDtypeStruct((t, D_MODEL), F32),
        scratch_shapes=[pltpu.VMEM((POST_TM, 2 * GROUP_W), BF16), pltpu.VMEM((POST_TM, D_MODEL), F32)],
        compiler_params=_cparams(("arbitrary",)),
        name="post_lat" if lat else "post_ctx",
    )(x, mod_l, pa, pb, pb, pb, yc, yd, *[lw[n] for n in mix_names], lw["w_out"], lw["g_post_mix"],
      lw["g_pre_ffn"], lw["g_post_ffn"], lw["w_gate_up"], lw["w_down"])


def _dft_tables(n, seq_len):
    scale = seq_len ** -0.5
    k = jnp.arange(n, dtype=jnp.int32)
    ang = ((k[:, None] * k[None, :]) % n).astype(F32) * (2.0 * math.pi / n)
    return (jnp.cos(ang) * scale).astype(BF16), (-jnp.sin(ang) * scale).astype(BF16)


def _twiddle_tables(n, sub):
    angs = []
    size = n
    while size > sub:
        angs.append(jnp.arange(size // 2, dtype=F32) * (2.0 * math.pi / size))
        size //= 2
    ang = jnp.concatenate(angs)[:, None]
    return jnp.broadcast_to(jnp.cos(ang), (ang.shape[0], LANES)), jnp.broadcast_to(jnp.sin(ang), (ang.shape[0], LANES))


def _interleave_perm(sub, nres):
    row = jnp.arange(sub, dtype=jnp.int32)
    src = (row % nres) * (sub // nres) + row // nres
    return (src[:, None] == row[None, :]).astype(BF16)


def _channel_dft():
    c = jnp.arange(C_CH, dtype=jnp.int32)
    ang = ((c[:, None] * c[None, :]) % C_CH).astype(F32) * (2.0 * math.pi / C_CH)
    eye = jnp.eye(C_GROUPS, dtype=F32)
    bc = jnp.kron(eye, jnp.cos(ang) * C_CH ** -0.5)
    bs = jnp.kron(eye, jnp.sin(ang) * C_CH ** -0.5)
    return jnp.concatenate([bc, bs], axis=1).astype(BF16)


def _rope_tables(n):
    pos = jnp.arange(n, dtype=jnp.int32)
    row = (pos // GRID_W).astype(F32)
    col = (pos % GRID_W).astype(F32)
    inv = ROPE_BASE ** (-jnp.arange(0, AXIS_ROPE, 2, dtype=F32) / AXIS_ROPE)
    half = AXIS_ROPE // 2
    ang = jnp.concatenate([row[:, None] * inv, row[:, None] * inv, col[:, None] * inv, col[:, None] * inv], axis=1)
    sign = jnp.tile(jnp.concatenate([-jnp.ones((half,), F32), jnp.ones((half,), F32)]), 2)
    cos = jnp.concatenate([jnp.cos(ang), jnp.ones((n, LANES - QK_ROPE), F32)], axis=1)
    sin = jnp.concatenate([jnp.sin(ang) * sign, jnp.zeros((n, LANES - QK_ROPE), F32)], axis=1)
    return cos, sin


def _layout_weights(g_pre_mix, g_post_mix, g_pre_ffn, g_post_ffn, w_in, spat_w, spat_b, conv_w, conv_b,
                    g_q_lora, w_uq, g_kv_lora, w_ukv, w_out, w_gate_up, w_down):
    d = DEPTH
    zeros = lambda *s: jnp.zeros(s, F32)
    c_kv0 = P_MAIN + Q_LORA
    c_kr0 = c_kv0 + KV_LORA
    lane_pad = lambda w, width: jnp.pad(w, ((0, 0), (0, 0), (0, width - w.shape[2]))).astype(BF16)
    uq = w_uq.reshape(d, Q_LORA, MLA_HEADS, QK_NOPE + QK_ROPE)
    wq4 = jnp.concatenate([uq[..., QK_NOPE:], uq[..., :QK_NOPE],
                           zeros(d, Q_LORA, MLA_HEADS, HEAD_PAD - QK_NOPE - QK_ROPE)], axis=3)
    wq4 = wq4.reshape(d, Q_LORA, QW)
    wq4 = jnp.concatenate([wq4, zeros(d, Q_LORA_PAD - Q_LORA, QW)], axis=1).astype(BF16)
    ukv = w_ukv.reshape(d, KV_LORA, MLA_HEADS, QK_NOPE + V_DIM)
    wk4 = jnp.concatenate([zeros(d, KV_LORA, MLA_HEADS, QK_ROPE), ukv[..., :QK_NOPE],
                           zeros(d, KV_LORA, MLA_HEADS, HEAD_PAD - QK_NOPE - QK_ROPE)], axis=3)
    wk4 = wk4.reshape(d, KV_LORA, QW).astype(BF16)
    wvt = ukv[..., QK_NOPE:].reshape(d, KV_LORA, VW).transpose(0, 2, 1).astype(BF16)
    g_q = jnp.concatenate([g_q_lora, zeros(d, Q_LORA_PAD - Q_LORA)], axis=1)
    return dict(
        g_pre_mix=g_pre_mix.reshape(d, 1, D_MODEL), g_post_mix=g_post_mix.reshape(d, 1, D_MODEL),
        g_pre_ffn=g_pre_ffn.reshape(d, 1, D_MODEL), g_post_ffn=g_post_ffn.reshape(d, 1, D_MODEL),
        w_main=w_in.astype(BF16),
        w_cq=lane_pad(w_in[:, :, P_MAIN:c_kv0], Q_LORA_PAD),
        w_ckv=w_in[:, :, c_kv0:c_kr0].astype(BF16),
        w_kr=lane_pad(w_in[:, :, c_kr0:], LANES),
        wq4=wq4, wk4=wk4, wvt=wvt,
        g_q=g_q.reshape(d, 1, Q_LORA_PAD), g_kv=g_kv_lora.reshape(d, 1, KV_LORA),
        ws=spat_w.reshape(d, A_GROUPS * CHUNK, CHUNK).astype(BF16),
        sb=jnp.repeat(spat_b.transpose(0, 2, 1), GROUP_W // A_GROUPS, axis=2),
        conv_w=conv_w, conv_b=conv_b.reshape(d, 1, GROUP_W),
        w_out=w_out.astype(BF16), w_gate_up=w_gate_up.astype(BF16), w_down=w_down.astype(BF16),
        bd=jnp.broadcast_to(_channel_dft(), (d, GROUP_W, 2 * GROUP_W)),
    )


def _trunk_layer(x, mod_l, lw, layer, lat, nb, seq_len, premix_tabs, dft_tabs, cache):
    outs = _premix(x, mod_l, lw, layer, lat, seq_len, premix_tabs)
    if lat:
        pa, pb, a, b, q, k, vt = outs
        yc = _posdft(a, b, dft_tabs, nb, seq_len)
        states = ()
    else:
        pa, pb, yc, q, k, vt = outs[:6]
        states = outs[6:]
    yd = _attention(q, k, vt, nb, seq_len, cache, layer)
    x = _post(x, mod_l, pa, pb, yc, yd, lw, layer, lat, seq_len)
    return x, states


def kernel(x_prompt, x_sample, cache_ckv, cache_krope, c, c_ctx, w_ada, b_ada, g_pre_mix, g_post_mix, g_pre_ffn,
           g_post_ffn, w_in, spat_w, spat_b, conv_w, conv_b, g_q_lora, w_uq, g_kv_lora, w_ukv, w_out, w_gate_up,
           w_down):
    batch, seq, _ = x_prompt.shape
    dec_batch, dec_seq, _ = x_sample.shape
    assert (batch * seq) % POST_TM == 0 and TM % seq == 0 and dec_seq % POST_TM == 0 and dec_batch + 1 <= 8

    stacked = _layout_weights(g_pre_mix, g_post_mix, g_pre_ffn, g_post_ffn, w_in, spat_w, spat_b, conv_w, conv_b,
                              g_q_lora, w_uq, g_kv_lora, w_ukv, w_out, w_gate_up, w_down)
    dft_ctx = _dft_tables(seq, seq)
    dft_sub = min(dec_seq, DFT_SUB)
    dft_lat = (_twiddle_tables(dec_seq, dft_sub) + _dft_tables(dft_sub, dec_seq)
               + (_interleave_perm(dft_sub, dec_seq // dft_sub),))
    rope_tabs = _rope_tables(dec_seq)

    cond8 = jnp.concatenate([c_ctx[None, :], c, jnp.zeros((8 - 1 - dec_batch, D_MODEL), F32)], axis=0)
    mod = _modulation(cond8, w_ada, b_ada).reshape(DEPTH, 8, 6, D_MODEL)

    cache_kr_pad = jnp.pad(cache_krope, ((0, 0), (0, 0), (0, 0), (0, LANES - QK_ROPE)))
    cache = _cache_kv(cache_ckv, cache_kr_pad, stacked["wk4"], stacked["wvt"])

    y_ctx = x_prompt.reshape(batch * seq, D_MODEL)
    y_lat = x_sample.reshape(dec_batch * dec_seq, D_MODEL)
    ckv_list, krope_list = [], []
    for l in range(DEPTH):
        y_ctx, (ckv_l, kr_l) = _trunk_layer(y_ctx, mod[l], stacked, l, False, batch, seq, dft_ctx, None, None)
        ckv_list.append(ckv_l.reshape(batch, seq, KV_LORA))
        krope_list.append(kr_l.reshape(batch, seq, QK_ROPE))
        y_lat, _ = _trunk_layer(y_lat, mod[l], stacked, l, True, dec_batch, dec_seq, rope_tabs, dft_lat, cache)
    state_ckv = jnp.stack(ckv_list, axis=1)
    state_krope = jnp.stack(krope_list, axis=1)
    return (y_ctx.reshape(batch, seq, D_MODEL), y_lat.reshape(dec_batch, dec_seq, D_MODEL), state_ckv, state_krope)
```

```python
import functools
import math

import jax
import jax.numpy as jnp
from jax import lax
from jax.experimental import pallas as pl
from jax.experimental.pallas import tpu as pltpu

F32 = jnp.float32
BF16 = jnp.bfloat16

D_MODEL = 1024
DEPTH = 4
GRID_W = 64
GROUP_W = 256
A_GROUPS = 4
CHUNK = 128
C_GROUPS = 4
C_CH = 64
MLA_HEADS = 4
QK_NOPE = 64
QK_ROPE = 32
V_DIM = 64
Q_LORA = 192
KV_LORA = 128
ROPE_BASE = 10000.0
AXIS_ROPE = QK_ROPE // 2
FF_HIDDEN = 2816
EPS = 1e-6

LANES = 128
HEAD_PAD = 128
Q_LORA_PAD = 256
P_MAIN = 512 + 768 + 256
QW = MLA_HEADS * HEAD_PAD
VW = MLA_HEADS * V_DIM
FF_CHUNK = 256
N_FF_CHUNKS = FF_HIDDEN // FF_CHUNK
TM = 512
POST_TM = 1024
TQ = 1024
ATT_SUB = 256
ATT_KC = 2048
ATT_EB = 128
DFT_SUB = 512
DFT_RB = 32
VMEM_LIMIT = 56 * 1024 * 1024
Q_SCALE = (QK_NOPE + QK_ROPE) ** -0.5 * math.log2(math.e)


def _cparams(sem, **flags):
    return pltpu.CompilerParams(dimension_semantics=sem, vmem_limit_bytes=VMEM_LIMIT, flags=flags or None)


def _layer_spec(w, layer, **kw):
    tail = w.shape[1:]
    nz = (0,) * len(tail)
    return pl.BlockSpec((None,) + tail, lambda *_: (layer,) + nz, **kw)


def _rms_scale(x, n):
    ms = jnp.sum(x * x, axis=-1, keepdims=True) * (1.0 / n)
    return x * lax.rsqrt(ms + EPS)


def _silu(x):
    return x * jax.nn.sigmoid(x)


def _dot(a, b):
    return jnp.dot(a, b, preferred_element_type=F32)


def _dot_nt(a, b):
    return lax.dot_general(a, b, (((1,), (1,)), ((), ())), preferred_element_type=F32)


def _mod_kernel(cond_ref, w_ref, b_ref, o_ref):
    s = _silu(cond_ref[...])
    o_ref[...] = _dot(s.astype(BF16), w_ref[...].astype(BF16)) + b_ref[...]


def _modulation(cond8, w_ada, b_ada):
    nt = 6 * D_MODEL // 1024
    return pl.pallas_call(
        _mod_kernel,
        grid=(DEPTH, nt),
        in_specs=[
            pl.BlockSpec((8, D_MODEL), lambda l, j: (0, 0)),
            pl.BlockSpec((None, D_MODEL, 1024), lambda l, j: (l, 0, j)),
            pl.BlockSpec((None, 1, 1024), lambda l, j: (l, 0, j)),
        ],
        out_specs=pl.BlockSpec((None, 8, 1024), lambda l, j: (l, 0, j)),
        out_shape=jax.ShapeDtypeStruct((DEPTH, 8, 6 * D_MODEL), F32),
        compiler_params=_cparams(("arbitrary", "arbitrary")),
        name="modulation",
    )(cond8, w_ada, b_ada.reshape(DEPTH, 1, 6 * D_MODEL))


def _cachekv_kernel(ckv_ref, kr_ref, wk_ref, wvt_ref, k_ref, vt_ref):
    c = ckv_ref[...].astype(BF16)
    kk = _dot(c, wk_ref[...])
    kr = kr_ref[...]
    for h in range(MLA_HEADS):
        k_ref[:, h * HEAD_PAD:(h + 1) * HEAD_PAD] = (kk[:, h * HEAD_PAD:(h + 1) * HEAD_PAD] + kr).astype(BF16)
    vt_ref[...] = _dot_nt(wvt_ref[...], c).astype(BF16)


def _cache_kv(cache_ckv, cache_kr_pad, wk4, wvt):
    nb, _, past, _ = cache_ckv.shape
    return pl.pallas_call(
        _cachekv_kernel,
        grid=(nb, DEPTH),
        in_specs=[
            pl.BlockSpec((None, None, past, KV_LORA), lambda b, l: (b, l, 0, 0)),
            pl.BlockSpec((None, None, past, LANES), lambda b, l: (b, l, 0, 0)),
            pl.BlockSpec((None, KV_LORA, QW), lambda b, l: (l, 0, 0)),
            pl.BlockSpec((None, VW, KV_LORA), lambda b, l: (l, 0, 0)),
        ],
        out_specs=[
            pl.BlockSpec((None, None, past, QW), lambda b, l: (b, l, 0, 0)),
            pl.BlockSpec((None, None, VW, past), lambda b, l: (b, l, 0, 0)),
        ],
        out_shape=[
            jax.ShapeDtypeStruct((nb, DEPTH, past, QW), BF16),
            jax.ShapeDtypeStruct((nb, DEPTH, VW, past), BF16),
        ],
        compiler_params=_cparams(("arbitrary", "arbitrary")),
        name="cache_kv",
    )(cache_ckv, cache_kr_pad, wk4, wvt)


def _rope(x, cos, sin):
    w = x.shape[1]
    lane = lax.broadcasted_iota(jnp.int32, x.shape, 1)
    up = pltpu.roll(x, w - 8, axis=1)
    dn = pltpu.roll(x, 8, axis=1)
    sw = jnp.where((lane & 15) < 8, up, dn)
    reps = w // LANES
    if reps > 1:
        cos = jnp.concatenate([cos] * reps, axis=1)
        sin = jnp.concatenate([sin] * reps, axis=1)
    return x * cos + sw * sin


def _premix_kernel(*refs, lat, seq_len):
    if lat:
        (x_ref, mod_ref, g_ref, win_ref, wcq_ref, wckv_ref, wkr_ref, bd_ref, gq_ref, wq_ref, gkv_ref, wk_ref,
         wvt_ref, cos_ref, sin_ref, pa_ref, pb_ref, a_ref, b_ref, q_ref, k_ref, vt_ref) = refs
    else:
        (x_ref, mod_ref, g_ref, win_ref, wcq_ref, wkv_ref, bd_ref, gq_ref, wq_ref, gkv_ref, wk_ref,
         wvt_ref, dc_ref, ds_ref, pa_ref, pb_ref, yc_ref, q_ref, k_ref, vt_ref, ckv_ref, kr_ref) = refs
    gain = g_ref[...] * (1.0 + mod_ref[1:2, :])
    h = _rms_scale(x_ref[...], D_MODEL) * gain + mod_ref[0:1, :]
    hb = h.astype(BF16)
    pa_ref[...] = _dot(hb, win_ref[:, 0:512]).astype(BF16)
    pb_ref[...] = _dot(hb, win_ref[:, 512:1280]).astype(BF16)
    pc = _dot(hb, win_ref[:, 1280:P_MAIN]).astype(BF16)
    ab = _dot(pc, bd_ref[...])
    if lat:
        a_ref[...] = ab[:, 0:GROUP_W]
        b_ref[...] = ab[:, GROUP_W:2 * GROUP_W]
    else:
        abb = ab.astype(BF16)
        for s in range(TM // seq_len):
            rows = slice(s * seq_len, (s + 1) * seq_len)
            yc_ref[rows, :] = (_dot(dc_ref[...], abb[rows, 0:GROUP_W])
                               + _dot(ds_ref[...], abb[rows, GROUP_W:2 * GROUP_W])).astype(BF16)
    cq = _dot(hb, wcq_ref[...])
    if lat:
        ckv = _dot(hb, wckv_ref[...])
        kr = _dot(hb, wkr_ref[...])
    else:
        kv = _dot(hb, wkv_ref[...])
        ckv = kv[:, 0:KV_LORA]
        kr = kv[:, KV_LORA:KV_LORA + LANES]
    qn = _rms_scale(cq, Q_LORA) * gq_ref[...]
    q4 = _dot(qn.astype(BF16), wq_ref[...])
    ckv_n = _rms_scale(ckv, KV_LORA) * gkv_ref[...]
    cb = ckv_n.astype(BF16)
    kk = _dot(cb, wk_ref[...])
    vt = _dot_nt(wvt_ref[...], cb).astype(BF16)
    if lat:
        vt_ref[...] = vt
        q4 = _rope(q4, cos_ref[...], sin_ref[...])
        kr = _rope(kr, cos_ref[...], sin_ref[...])
    else:
        for s in range(TM // seq_len):
            vt_ref[s] = vt[:, s * seq_len:(s + 1) * seq_len]
        ckv_ref[...] = ckv_n
        kr_ref[...] = kr[:, 0:QK_ROPE]
    q_ref[...] = (q4 * Q_SCALE).astype(BF16)
    for hd in range(MLA_HEADS):
        k_ref[:, hd * HEAD_PAD:(hd + 1) * HEAD_PAD] = (kk[:, hd * HEAD_PAD:(hd + 1) * HEAD_PAD] + kr).astype(BF16)


def _premix(x, mod_l, lw, layer, lat, seq_len, tabs):
    t = x.shape[0]
    nt = t // TM
    nb = t // seq_len
    tpb = max(seq_len // TM, 1)
    row = lambda i: (i, 0)
    if lat:
        mod_map = lambda i: (1 + i // tpb, 0, 0)
        vt_spec = pl.BlockSpec((None, VW, TM), lambda i: (i // tpb, 0, i % tpb))
        tab_specs = [pl.BlockSpec((TM, LANES), lambda i: (i % tpb, 0))] * 2
        fourier_specs = [pl.BlockSpec((TM, GROUP_W), row)] * 2
        fourier_shapes = [jax.ShapeDtypeStruct((t, GROUP_W), F32)] * 2
    else:
        mod_map = lambda i: (0, 0, 0)
        vt_spec = pl.BlockSpec((TM // seq_len, VW, seq_len), lambda i: (i, 0, 0))
        tab_specs = [pl.BlockSpec((seq_len, seq_len), lambda i: (0, 0))] * 2
        fourier_specs = [pl.BlockSpec((TM, GROUP_W), row)]
        fourier_shapes = [jax.ShapeDtypeStruct((t, GROUP_W), BF16)]
    kv_names = ["w_ckv", "w_kr"] if lat else ["w_kv"]
    names = ["g_pre_mix", "w_main", "w_cq"] + kv_names + ["bd", "g_q", "wq4", "g_kv", "wk4", "wvt"]
    in_specs = [pl.BlockSpec((TM, D_MODEL), row), pl.BlockSpec((None, 6, D_MODEL), mod_map)]
    in_specs += [_layer_spec(lw[n], layer) for n in names] + tab_specs
    args = [x, mod_l] + [lw[n] for n in names] + list(tabs)
    out_specs = [pl.BlockSpec((TM, 512), row), pl.BlockSpec((TM, 768), row)] + fourier_specs + [
        pl.BlockSpec((TM, QW), row),
        pl.BlockSpec((TM, QW), row),
        vt_spec,
    ]
    out_shape = [jax.ShapeDtypeStruct((t, 512), BF16), jax.ShapeDtypeStruct((t, 768), BF16)] + fourier_shapes + [
        jax.ShapeDtypeStruct((t, QW), BF16),
        jax.ShapeDtypeStruct((t, QW), BF16),
        jax.ShapeDtypeStruct((nb, VW, seq_len), BF16),
    ]
    if not lat:
        out_specs += [pl.BlockSpec((TM, KV_LORA), row), pl.BlockSpec((TM, QK_ROPE), row)]
        out_shape += [jax.ShapeDtypeStruct((t, KV_LORA), F32), jax.ShapeDtypeStruct((t, QK_ROPE), F32)]
    return pl.pallas_call(
        functools.partial(_premix_kernel, lat=lat, seq_len=seq_len),
        grid=(nt,),
        in_specs=in_specs,
        out_specs=out_specs,
        out_shape=out_shape,
        compiler_params=_cparams(("arbitrary",)),
        name="premix_lat" if lat else "premix_ctx",
    )(*args)


def _seqmix_chunks(pa_ref, pb_ref, prev_ref, next_ref, ws_ref, sb_ref, cw_ref, cb_ref, o_ref, chunks, seq_len):
    i = pl.program_id(0)
    n_chunks = POST_TM // CHUNK
    grp = lax.broadcasted_iota(jnp.int32, (CHUNK, GROUP_W), 1) // (GROUP_W // A_GROUPS)
    r = lax.broadcasted_iota(jnp.int32, (CHUNK, GROUP_W), 0)

    def z_row(ref, row):
        blk = slice(row // 16 * 16, row // 16 * 16 + 16)
        zz = ref[blk, 2 * GROUP_W:3 * GROUP_W].astype(F32) * ref[blk, 0:GROUP_W].astype(F32)
        return zz[row % 16:row % 16 + 1]

    tile_pos = (i * POST_TM) & (seq_len - 1)
    for c in chunks:
        rows = slice(c * CHUNK, (c + 1) * CHUNK)
        u = pa_ref[rows, 0:GROUP_W].astype(F32)
        v = pa_ref[rows, GROUP_W:2 * GROUP_W]
        m4 = _dot(ws_ref[...], v)
        mixed = m4[0:CHUNK]
        for g in range(1, A_GROUPS):
            mixed = jnp.where(grp == g, m4[g * CHUNK:(g + 1) * CHUNK], mixed)
        o_ref[rows, 0:GROUP_W] = (u * (mixed + sb_ref[...])).astype(BF16)
        z = pb_ref[rows, 2 * GROUP_W:3 * GROUP_W].astype(F32) * pb_ref[rows, 0:GROUP_W].astype(F32)
        if c > 0:
            before = z_row(pb_ref, c * CHUNK - 1) if (c * CHUNK) % seq_len else jnp.zeros((1, GROUP_W), F32)
        else:
            before = jnp.where(tile_pos != 0, z_row(prev_ref, 15), 0.0)
        if c < n_chunks - 1:
            after = z_row(pb_ref, (c + 1) * CHUNK) if ((c + 1) * CHUNK) % seq_len else jnp.zeros((1, GROUP_W), F32)
        else:
            after = jnp.where(((tile_pos + POST_TM) & (seq_len - 1)) != 0, z_row(next_ref, 0), 0.0)
        z_prev = jnp.where(r == 0, before, pltpu.roll(z, 1, axis=0))
        z_next = jnp.where(r == CHUNK - 1, after, pltpu.roll(z, CHUNK - 1, axis=0))
        y = z_prev * cw_ref[0:1, :] + z * cw_ref[1:2, :] + z_next * cw_ref[2:3, :] + cb_ref[...]
        gb = pb_ref[rows, GROUP_W:2 * GROUP_W].astype(F32)
        o_ref[rows, GROUP_W:2 * GROUP_W] = (gb * y).astype(BF16)


def _posdft_kernel(a_ref, b_ref, twc_ref, tws_ref, dc_ref, ds_ref, perm_ref, o_ref, x_scr, y_scr, res_scr, *,
                   levels):
    n = a_ref.shape[0]
    tw0 = 0
    for lv in range(levels):
        size = n >> lv
        half = size // 2
        src_x, src_y = (a_ref, b_ref) if lv == 0 else (x_scr, y_scr)
        for base in range(0, n, size):
            for j0 in range(0, half, DFT_RB):
                top = slice(base + j0, base + j0 + DFT_RB)
                bot = slice(base + half + j0, base + half + j0 + DFT_RB)
                xt, xb, yt, yb = src_x[top, :], src_x[bot, :], src_y[top, :], src_y[bot, :]
                c = twc_ref[tw0 + j0:tw0 + j0 + DFT_RB, :]
                s = tws_ref[tw0 + j0:tw0 + j0 + DFT_RB, :]
                c = jnp.concatenate([c] * (GROUP_W // LANES), axis=1)
                s = jnp.concatenate([s] * (GROUP_W // LANES), axis=1)
                x_scr[top, :] = xt + xb
                y_scr[top, :] = yt + yb
                dx, dy = xt - xb, yt - yb
                x_scr[bot, :] = dx * c - dy * s
                y_scr[bot, :] = dx * s + dy * c
        tw0 += half
    sub = n >> levels
    nres = 1 << levels
    for p in range(nres):
        r = int(format(p, "0%db" % levels)[::-1], 2) if levels else 0
        rows = slice(p * sub, (p + 1) * sub)
        xb = x_scr[rows, :].astype(BF16)
        yb = y_scr[rows, :].astype(BF16)
        res_scr[r] = (_dot(dc_ref[...], xb) + _dot(ds_ref[...], yb)).astype(BF16)
    grp = sub // nres
    for g in range(nres):
        stack = jnp.concatenate([res_scr[r, g * grp:(g + 1) * grp, :] for r in range(nres)], axis=0)
        o_ref[g * sub:(g + 1) * sub, :] = _dot(perm_ref[...], stack).astype(BF16)


def _posdft(a, b, tabs, nb, n):
    twc, tws, dc, ds, perm = tabs
    sub = dc.shape[0]
    nres = n // sub
    levels = nres.bit_length() - 1
    const = lambda i: (0, 0)
    return pl.pallas_call(
        functools.partial(_posdft_kernel, levels=levels),
        grid=(nb,),
        in_specs=[
            pl.BlockSpec((n, GROUP_W), lambda i: (i, 0)),
            pl.BlockSpec((n, GROUP_W), lambda i: (i, 0)),
            pl.BlockSpec(twc.shape, const),
            pl.BlockSpec(tws.shape, const),
            pl.BlockSpec((sub, sub), const),
            pl.BlockSpec((sub, sub), const),
            pl.BlockSpec((sub, sub), const),
        ],
        out_specs=pl.BlockSpec((n, GROUP_W), lambda i: (i, 0)),
        out_shape=jax.ShapeDtypeStruct((nb * n, GROUP_W), BF16),
        scratch_shapes=[pltpu.VMEM((n, GROUP_W), F32), pltpu.VMEM((n, GROUP_W), F32),
                        pltpu.VMEM((nres, sub, GROUP_W), BF16)],
        compiler_params=_cparams(("arbitrary",)),
        name="posdft",
    )(a, b, twc, tws, dc, ds, perm)


def _attn_kernel(*refs, has_cache):
    if has_cache:
        q_ref, k_ref, vt_ref, kc_ref, vct_ref, o_ref, st_scr, pt_scr = refs
    else:
        q_ref, k_ref, vt_ref, o_ref, st_scr, pt_scr = refs
    nk = k_ref.shape[0]
    kc = min(nk, ATT_KC)
    segs = [(k_ref, vt_ref, c * kc, c * kc, kc) for c in range(nk // kc)]
    if has_cache:
        segs.append((kc_ref, vct_ref, 0, nk, kc_ref.shape[0]))
    nsub = q_ref.shape[0] // ATT_SUB
    items = [(s, h) for s in range(nsub) for h in range(MLA_HEADS)]
    outs = [[None] * MLA_HEADS for _ in range(nsub)]
    m_prev = None
    for t in range(len(items) + 1):
        if t < len(items):
            s, h = items[t]
            cols = slice(h * HEAD_PAD, (h + 1) * HEAD_PAD)
            qh = q_ref[s * ATT_SUB:(s + 1) * ATT_SUB, cols]
            m = None
        if t >= 1:
            ps, ph = items[t - 1]
            vrows = slice(ph * V_DIM, (ph + 1) * V_DIM)
            l = jnp.zeros((1, ATT_SUB), F32)
            ot = jnp.zeros((V_DIM, ATT_SUB), F32)
        for ci, (kref, vref, r0, s0, n) in enumerate(segs):
            if t < len(items):
                st = _dot_nt(kref[r0:r0 + n, cols], qh)
                st_scr[t % 2, s0:s0 + n, :] = st
                for j in range(0, n, ATT_EB):
                    cm = jnp.max(st[j:j + ATT_EB], axis=0, keepdims=True)
                    m = cm if m is None else jnp.maximum(m, cm)
            if t >= 1:
                for j in range(0, n, ATT_EB):
                    p = jnp.exp2(st_scr[(t - 1) % 2, s0 + j:s0 + j + ATT_EB, :] - m_prev)
                    l = l + jnp.sum(p, axis=0, keepdims=True)
                    pt_scr[ci % 2, j:j + ATT_EB, :] = p.astype(BF16)
                ot = ot + _dot(vref[vrows, r0:r0 + n], pt_scr[ci % 2, 0:n, :])
        if t >= 1:
            outs[ps][ph] = ot / l
        if t < len(items):
            m_prev = m
    for s in range(nsub):
        o_ref[s * ATT_SUB:(s + 1) * ATT_SUB, :] = jnp.concatenate(outs[s], axis=0).T.astype(BF16)


def _attention(q, k, vt, nb, seq_len, cache=None, layer=0):
    tq = min(seq_len, TQ)
    nk_total = seq_len + (cache[0].shape[2] if cache is not None else 0)
    q3 = q.reshape(nb, seq_len, QW)
    k3 = k.reshape(nb, seq_len, QW)
    in_specs = [
        pl.BlockSpec((None, tq, QW), lambda b, i: (b, i, 0)),
        pl.BlockSpec((None, seq_len, QW), lambda b, i: (b, 0, 0)),
        pl.BlockSpec((None, VW, seq_len), lambda b, i: (b, 0, 0)),
    ]
    args = [q3, k3, vt]
    if cache is not None:
        kc, vct = cache
        past = kc.shape[2]
        in_specs += [
            pl.BlockSpec((None, None, past, QW), lambda b, i: (b, layer, 0, 0)),
            pl.BlockSpec((None, None, VW, past), lambda b, i: (b, layer, 0, 0)),
        ]
        args += [kc, vct]
    out = pl.pallas_call(
        functools.partial(_attn_kernel, has_cache=cache is not None),
        grid=(nb, seq_len // tq),
        in_specs=in_specs,
        out_specs=pl.BlockSpec((None, tq, VW), lambda b, i: (b, i, 0)),
        out_shape=jax.ShapeDtypeStruct((nb, seq_len, VW), BF16),
        scratch_shapes=[pltpu.VMEM((2, nk_total, ATT_SUB), F32),
                        pltpu.VMEM((2, min(seq_len, ATT_KC), ATT_SUB), BF16)],
        compiler_params=_cparams(("arbitrary", "arbitrary")),
        name="attn_lat" if cache is not None else "attn_ctx",
    )(*args)
    return out.reshape(nb * seq_len, VW)


def _post_kernel(x_ref, mod_ref, pa_ref, pb_ref, prev_ref, next_ref, yc_ref, yd_ref, ws_ref, sb_ref, cw_ref, cb_ref,
                 wo_ref, gpm_ref, gpf_ref, gpo_ref, wgu_ref, wd_ref, o_ref, yab_scr, acc_scr, *, seq_len):
    for hf in range(POST_TM // TM):
        rows = slice(hf * TM, (hf + 1) * TM)
        _seqmix_chunks(pa_ref, pb_ref, prev_ref, next_ref, ws_ref, sb_ref, cw_ref, cb_ref, yab_scr,
                       range(hf * (TM // CHUNK), (hf + 1) * (TM // CHUNK)), seq_len)
        mix = (_dot(yab_scr[rows, :], wo_ref[0:2 * GROUP_W, :])
               + _dot(yc_ref[rows, :], wo_ref[2 * GROUP_W:3 * GROUP_W, :])
               + _dot(yd_ref[rows, :], wo_ref[3 * GROUP_W:4 * GROUP_W, :]))
        x1 = x_ref[rows, :] + _rms_scale(mix, D_MODEL) * (mod_ref[2:3, :] * gpm_ref[...])
        o_ref[rows, :] = x1
        h2 = _rms_scale(x1, D_MODEL) * (gpf_ref[...] * (1.0 + mod_ref[4:5, :])) + mod_ref[3:4, :]
        hb = h2.astype(BF16)
        for c in range(N_FF_CHUNKS):
            g = _dot(hb, wgu_ref[:, c * FF_CHUNK:(c + 1) * FF_CHUNK])
            u = _dot(hb, wgu_ref[:, FF_HIDDEN + c * FF_CHUNK:FF_HIDDEN + (c + 1) * FF_CHUNK])
            act = (_silu(g) * u).astype(BF16)
            d = _dot(act, wd_ref[c * FF_CHUNK:(c + 1) * FF_CHUNK, :])
            if c == 0:
                acc_scr[rows, :] = d
            else:
                acc_scr[rows, :] += d
        o_ref[rows, :] = o_ref[rows, :] + _rms_scale(acc_scr[rows, :], D_MODEL) * (mod_ref[5:6, :] * gpo_ref[...])


def _post(x, mod_l, pa, pb, yc, yd, lw, layer, lat, seq_len):
    t = x.shape[0]
    nt = t // POST_TM
    tpb = max(seq_len // POST_TM, 1)
    mod_map = (lambda i: (1 + i // tpb, 0, 0)) if lat else (lambda i: (0, 0, 0))
    row = lambda i: (i, 0)
    hb = POST_TM // 16
    last = t // 16 - 1
    resident = dict(pipeline_mode=pl.Buffered(1))
    mix_names = ["ws", "sb", "conv_w", "conv_b"]
    return pl.pallas_call(
        functools.partial(_post_kernel, seq_len=seq_len),
        grid=(nt,),
        in_specs=[
            pl.BlockSpec((POST_TM, D_MODEL), row),
            pl.BlockSpec((None, 6, D_MODEL), mod_map),
            pl.BlockSpec((POST_TM, 512), row),
            pl.BlockSpec((POST_TM, 768), row),
            pl.BlockSpec((16, 768), lambda i: (jnp.maximum(i * hb - 1, 0), 0)),
            pl.BlockSpec((16, 768), lambda i: (jnp.minimum((i + 1) * hb, last), 0)),
            pl.BlockSpec((POST_TM, GROUP_W), row),
            pl.BlockSpec((POST_TM, GROUP_W), row),
        ] + [_layer_spec(lw[n], layer) for n in mix_names] + [
            _layer_spec(lw["w_out"], layer, **resident),
            _layer_spec(lw["g_post_mix"], layer),
            _layer_spec(lw["g_pre_ffn"], layer),
            _layer_spec(lw["g_post_ffn"], layer),
            _layer_spec(lw["w_gate_up"], layer, **resident),
            _layer_spec(lw["w_down"], layer, **resident),
        ],
        out_specs=pl.BlockSpec((POST_TM, D_MODEL), row),
        out_shape=jax.ShapeDtypeStruct((t, D_MODEL), F32),
        scratch_shapes=[pltpu.VMEM((POST_TM, 2 * GROUP_W), BF16), pltpu.VMEM((POST_TM, D_MODEL), F32)],
        compiler_params=_cparams(("arbitrary",)),
        name="post_lat" if lat else "post_ctx",
    )(x, mod_l, pa, pb, pb, pb, yc, yd, *[lw[n] for n in mix_names], lw["w_out"], lw["g_post_mix"],
      lw["g_pre_ffn"], lw["g_post_ffn"], lw["w_gate_up"], lw["w_down"])


def _dft_tables(n, seq_len):
    scale = seq_len ** -0.5
    k = jnp.arange(n, dtype=jnp.int32)
    ang = ((k[:, None] * k[None, :]) % n).astype(F32) * (2.0 * math.pi / n)
    return (jnp.cos(ang) * scale).astype(BF16), (-jnp.sin(ang) * scale).astype(BF16)


def _twiddle_tables(n, sub):
    angs = []
    size = n
    while size > sub:
        angs.append(jnp.arange(size // 2, dtype=F32) * (2.0 * math.pi / size))
        size //= 2
    ang = jnp.concatenate(angs)[:, None]
    return jnp.broadcast_to(jnp.cos(ang), (ang.shape[0], LANES)), jnp.broadcast_to(jnp.sin(ang), (ang.shape[0], LANES))


def _interleave_perm(sub, nres):
    row = jnp.arange(sub, dtype=jnp.int32)
    src = (row % nres) * (sub // nres) + row // nres
    return (src[:, None] == row[None, :]).astype(BF16)


def _channel_dft():
    c = jnp.arange(C_CH, dtype=jnp.int32)
    ang = ((c[:, None] * c[None, :]) % C_CH).astype(F32) * (2.0 * math.pi / C_CH)
    eye = jnp.eye(C_GROUPS, dtype=F32)
    bc = jnp.kron(eye, jnp.cos(ang) * C_CH ** -0.5)
    bs = jnp.kron(eye, jnp.sin(ang) * C_CH ** -0.5)
    return jnp.concatenate([bc, bs], axis=1).astype(BF16)


def _rope_tables(n):
    pos = jnp.arange(n, dtype=jnp.int32)
    row = (pos // GRID_W).astype(F32)
    col = (pos % GRID_W).astype(F32)
    inv = ROPE_BASE ** (-jnp.arange(0, AXIS_ROPE, 2, dtype=F32) / AXIS_ROPE)
    half = AXIS_ROPE // 2
    ang = jnp.concatenate([row[:, None] * inv, row[:, None] * inv, col[:, None] * inv, col[:, None] * inv], axis=1)
    sign = jnp.tile(jnp.concatenate([-jnp.ones((half,), F32), jnp.ones((half,), F32)]), 2)
    cos = jnp.concatenate([jnp.cos(ang), jnp.ones((n, LANES - QK_ROPE), F32)], axis=1)
    sin = jnp.concatenate([jnp.sin(ang) * sign, jnp.zeros((n, LANES - QK_ROPE), F32)], axis=1)
    return cos, sin


def _layout_weights(g_pre_mix, g_post_mix, g_pre_ffn, g_post_ffn, w_in, spat_w, spat_b, conv_w, conv_b,
                    g_q_lora, w_uq, g_kv_lora, w_ukv, w_out, w_gate_up, w_down):
    d = DEPTH
    zeros = lambda *s: jnp.zeros(s, F32)
    c_kv0 = P_MAIN + Q_LORA
    lane_pad = lambda w, width: jnp.pad(w, ((0, 0), (0, 0), (0, width - w.shape[2]))).astype(BF16)
    uq = w_uq.reshape(d, Q_LORA, MLA_HEADS, QK_NOPE + QK_ROPE)
    wq4 = jnp.concatenate([uq[..., QK_NOPE:], uq[..., :QK_NOPE],
                           zeros(d, Q_LORA, MLA_HEADS, HEAD_PAD - QK_NOPE - QK_ROPE)], axis=3)
    wq4 = wq4.reshape(d, Q_LORA, QW)
    wq4 = jnp.concatenate([wq4, zeros(d, Q_LORA_PAD - Q_LORA, QW)], axis=1).astype(BF16)
    ukv = w_ukv.reshape(d, KV_LORA, MLA_HEADS, QK_NOPE + V_DIM)
    wk4 = jnp.concatenate([zeros(d, KV_LORA, MLA_HEADS, QK_ROPE), ukv[..., :QK_NOPE],
                           zeros(d, KV_LORA, MLA_HEADS, HEAD_PAD - QK_NOPE - QK_ROPE)], axis=3)
    wk4 = wk4.reshape(d, KV_LORA, QW).astype(BF16)
    wvt = ukv[..., QK_NOPE:].reshape(d, KV_LORA, VW).transpose(0, 2, 1).astype(BF16)
    g_q = jnp.concatenate([g_q_lora, zeros(d, Q_LORA_PAD - Q_LORA)], axis=1)
    return dict(
        g_pre_mix=g_pre_mix.reshape(d, 1, D_MODEL), g_post_mix=g_post_mix.reshape(d, 1, D_MODEL),
        g_pre_ffn=g_pre_ffn.reshape(d, 1, D_MODEL), g_post_ffn=g_post_ffn.reshape(d, 1, D_MODEL),
        w_main=w_in.astype(BF16),
        w_cq=lane_pad(w_in[:, :, P_MAIN:c_kv0], Q_LORA_PAD),
        w_kv=lane_pad(w_in[:, :, c_kv0:], KV_LORA + LANES),
        w_ckv=w_in[:, :, c_kv0:c_kv0 + KV_LORA].astype(BF16),
        w_kr=lane_pad(w_in[:, :, c_kv0 + KV_LORA:], LANES),
        wq4=wq4, wk4=wk4, wvt=wvt,
        g_q=g_q.reshape(d, 1, Q_LORA_PAD), g_kv=g_kv_lora.reshape(d, 1, KV_LORA),
        ws=spat_w.reshape(d, A_GROUPS * CHUNK, CHUNK).astype(BF16),
        sb=jnp.repeat(spat_b.transpose(0, 2, 1), GROUP_W // A_GROUPS, axis=2),
        conv_w=conv_w, conv_b=conv_b.reshape(d, 1, GROUP_W),
        w_out=w_out.astype(BF16), w_gate_up=w_gate_up.astype(BF16), w_down=w_down.astype(BF16),
        bd=jnp.broadcast_to(_channel_dft(), (d, GROUP_W, 2 * GROUP_W)),
    )


def _trunk_layer(x, mod_l, lw, layer, lat, nb, seq_len, premix_tabs, dft_tabs, cache):
    outs = _premix(x, mod_l, lw, layer, lat, seq_len, premix_tabs)
    if lat:
        pa, pb, a, b, q, k, vt = outs
        yc = _posdft(a, b, dft_tabs, nb, seq_len)
        states = ()
    else:
        pa, pb, yc, q, k, vt = outs[:6]
        states = outs[6:]
    yd = _attention(q, k, vt, nb, seq_len, cache, layer)
    x = _post(x, mod_l, pa, pb, yc, yd, lw, layer, lat, seq_len)
    return x, states


def kernel(x_prompt, x_sample, cache_ckv, cache_krope, c, c_ctx, w_ada, b_ada, g_pre_mix, g_post_mix, g_pre_ffn,
           g_post_ffn, w_in, spat_w, spat_b, conv_w, conv_b, g_q_lora, w_uq, g_kv_lora, w_ukv, w_out, w_gate_up,
           w_down):
    batch, seq, _ = x_prompt.shape
    dec_batch, dec_seq, _ = x_sample.shape
    assert (batch * seq) % POST_TM == 0 and TM % seq == 0 and dec_seq % POST_TM == 0 and dec_batch + 1 <= 8

    stacked = _layout_weights(g_pre_mix, g_post_mix, g_pre_ffn, g_post_ffn, w_in, spat_w, spat_b, conv_w, conv_b,
                              g_q_lora, w_uq, g_kv_lora, w_ukv, w_out, w_gate_up, w_down)
    dft_ctx = _dft_tables(seq, seq)
    dft_sub = min(dec_seq, DFT_SUB)
    dft_lat = (_twiddle_tables(dec_seq, dft_sub) + _dft_tables(dft_sub, dec_seq)
               + (_interleave_perm(dft_sub, dec_seq // dft_sub),))
    rope_tabs = _rope_tables(dec_seq)

    cond8 = jnp.concatenate([c_ctx[None, :], c, jnp.zeros((8 - 1 - dec_batch, D_MODEL), F32)], axis=0)
    mod = _modulation(cond8, w_ada, b_ada).reshape(DEPTH, 8, 6, D_MODEL)

    cache_kr_pad = jnp.pad(cache_krope, ((0, 0), (0, 0), (0, 0), (0, LANES - QK_ROPE)))
    cache = _cache_kv(cache_ckv, cache_kr_pad, stacked["wk4"], stacked["wvt"])

    y_ctx = x_prompt.reshape(batch * seq, D_MODEL)
    y_lat = x_sample.reshape(dec_batch * dec_seq, D_MODEL)
    ckv_list, krope_list = [], []
    for l in range(DEPTH):
        y_ctx, (ckv_l, kr_l) = _trunk_layer(y_ctx, mod[l], stacked, l, False, batch, seq, dft_ctx, None, None)
        ckv_list.append(ckv_l.reshape(batch, seq, KV_LORA))
        krope_list.append(kr_l.reshape(batch, seq, QK_ROPE))
        y_lat, _ = _trunk_layer(y_lat, mod[l], stacked, l, True, dec_batch, dec_seq, rope_tabs, dft_lat, cache)
    state_ckv = jnp.stack(ckv_list, axis=1)
    state_krope = jnp.stack(krope_list, axis=1)
    return (y_ctx.reshape(batch, seq, D_MODEL), y_lat.reshape(dec_batch, dec_seq, D_MODEL), state_ckv, state_krope)
```

```python
import functools
import math

import jax
import jax.numpy as jnp
from jax import lax
from jax.experimental import pallas as pl
from jax.experimental.pallas import tpu as pltpu

F32 = jnp.float32
BF16 = jnp.bfloat16

D_MODEL = 1024
DEPTH = 4
GRID_W = 64
GROUP_W = 256
A_GROUPS = 4
CHUNK = 128
C_GROUPS = 4
C_CH = 64
MLA_HEADS = 4
QK_NOPE = 64
QK_ROPE = 32
V_DIM = 64
Q_LORA = 192
KV_LORA = 128
ROPE_BASE = 10000.0
AXIS_ROPE = QK_ROPE // 2
FF_HIDDEN = 2816
EPS = 1e-6

LANES = 128
HEAD_PAD = 128
Q_LORA_PAD = 256
P_MAIN = 512 + 768 + 256
QW = MLA_HEADS * HEAD_PAD
VW = MLA_HEADS * V_DIM
FF_CHUNK = 256
N_FF_CHUNKS = FF_HIDDEN // FF_CHUNK
TM = 512
POST_TM = 1024
PRE_TM = 1024
TQ = 512
ATT_SUB = 256
ATT_KC = 2048
ATT_EB = 128
DFT_SUB = 512
DFT_RB = 32
VMEM_LIMIT = 56 * 1024 * 1024
Q_SCALE = (QK_NOPE + QK_ROPE) ** -0.5 * math.log2(math.e)


def _cparams(sem, **flags):
    return pltpu.CompilerParams(dimension_semantics=sem, vmem_limit_bytes=VMEM_LIMIT, flags=flags or None)


def _layer_spec(w, layer, **kw):
    tail = w.shape[1:]
    nz = (0,) * len(tail)
    return pl.BlockSpec((None,) + tail, lambda *_: (layer,) + nz, **kw)


def _rms_scale(x, n):
    ms = jnp.sum(x * x, axis=-1, keepdims=True) * (1.0 / n)
    return x * lax.rsqrt(ms + EPS)


def _silu(x):
    return x * jax.nn.sigmoid(x)


def _dot(a, b):
    return jnp.dot(a, b, preferred_element_type=F32)


def _dot_nt(a, b):
    return lax.dot_general(a, b, (((1,), (1,)), ((), ())), preferred_element_type=F32)


def _mod_kernel(cond_ref, w_ref, b_ref, o_ref):
    s = _silu(cond_ref[...])
    o_ref[...] = _dot(s.astype(BF16), w_ref[...].astype(BF16)) + b_ref[...]


def _modulation(cond8, w_ada, b_ada):
    nt = 6 * D_MODEL // 1024
    return pl.pallas_call(
        _mod_kernel,
        grid=(DEPTH, nt),
        in_specs=[
            pl.BlockSpec((8, D_MODEL), lambda l, j: (0, 0)),
            pl.BlockSpec((None, D_MODEL, 1024), lambda l, j: (l, 0, j)),
            pl.BlockSpec((None, 1, 1024), lambda l, j: (l, 0, j)),
        ],
        out_specs=pl.BlockSpec((None, 8, 1024), lambda l, j: (l, 0, j)),
        out_shape=jax.ShapeDtypeStruct((DEPTH, 8, 6 * D_MODEL), F32),
        compiler_params=_cparams(("arbitrary", "arbitrary")),
        name="modulation",
    )(cond8, w_ada, b_ada.reshape(DEPTH, 1, 6 * D_MODEL))


def _cachekv_kernel(ckv_ref, kr_ref, wk_ref, wvt_ref, k_ref, vt_ref):
    c = ckv_ref[...].astype(BF16)
    kk = _dot(c, wk_ref[...])
    kr = kr_ref[...]
    for h in range(MLA_HEADS):
        k_ref[:, h * HEAD_PAD:(h + 1) * HEAD_PAD] = (kk[:, h * HEAD_PAD:(h + 1) * HEAD_PAD] + kr).astype(BF16)
    vt_ref[...] = _dot_nt(wvt_ref[...], c).astype(BF16)


def _cache_kv(cache_ckv, cache_kr_pad, wk4, wvt):
    nb, _, past, _ = cache_ckv.shape
    return pl.pallas_call(
        _cachekv_kernel,
        grid=(nb, DEPTH),
        in_specs=[
            pl.BlockSpec((None, None, past, KV_LORA), lambda b, l: (b, l, 0, 0)),
            pl.BlockSpec((None, None, past, LANES), lambda b, l: (b, l, 0, 0)),
            pl.BlockSpec((None, KV_LORA, QW), lambda b, l: (l, 0, 0)),
            pl.BlockSpec((None, VW, KV_LORA), lambda b, l: (l, 0, 0)),
        ],
        out_specs=[
            pl.BlockSpec((None, None, past, QW), lambda b, l: (b, l, 0, 0)),
            pl.BlockSpec((None, None, VW, past), lambda b, l: (b, l, 0, 0)),
        ],
        out_shape=[
            jax.ShapeDtypeStruct((nb, DEPTH, past, QW), BF16),
            jax.ShapeDtypeStruct((nb, DEPTH, VW, past), BF16),
        ],
        compiler_params=_cparams(("arbitrary", "arbitrary")),
        name="cache_kv",
    )(cache_ckv, cache_kr_pad, wk4, wvt)


def _rope(x, cos, sin):
    w = x.shape[1]
    lane = lax.broadcasted_iota(jnp.int32, x.shape, 1)
    up = pltpu.roll(x, w - 8, axis=1)
    dn = pltpu.roll(x, 8, axis=1)
    sw = jnp.where((lane & 15) < 8, up, dn)
    reps = w // LANES
    if reps > 1:
        cos = jnp.concatenate([cos] * reps, axis=1)
        sin = jnp.concatenate([sin] * reps, axis=1)
    return x * cos + sw * sin


def _premix_kernel(*refs, lat, seq_len):
    if lat:
        (x_ref, mod_ref, g_ref, win_ref, wcq_ref, wckv_ref, wkr_ref, bd_ref, gq_ref, wq_ref, gkv_ref, wk_ref,
         wvt_ref, cos_ref, sin_ref, pa_ref, pb_ref, a_ref, b_ref, q_ref, k_ref, vt_ref) = refs[:-1]
    else:
        (x_ref, mod_ref, g_ref, win_ref, wcq_ref, wkv_ref, bd_ref, gq_ref, wq_ref, gkv_ref, wk_ref,
         wvt_ref, dc_ref, ds_ref, pa_ref, pb_ref, yc_ref, q_ref, k_ref, vt_ref, ckv_ref, kr_ref) = refs[:-1]
    hb_scr = refs[-1]
    halves = [slice(hf * TM, (hf + 1) * TM) for hf in range(PRE_TM // TM)]
    gain = g_ref[...] * (1.0 + mod_ref[1:2, :])
    for rows in halves:
        h = _rms_scale(x_ref[rows, :], D_MODEL) * gain + mod_ref[0:1, :]
        hb_scr[rows, :] = h.astype(BF16)
    for hf, rows in enumerate(halves):
        hb = hb_scr[rows, :]
        pa_ref[rows, :] = _dot(hb, win_ref[:, 0:512]).astype(BF16)
        pb_ref[rows, :] = _dot(hb, win_ref[:, 512:1280]).astype(BF16)
        pc = _dot(hb, win_ref[:, 1280:P_MAIN]).astype(BF16)
        ab = _dot(pc, bd_ref[...])
        if lat:
            a_ref[rows, :] = ab[:, 0:GROUP_W]
            b_ref[rows, :] = ab[:, GROUP_W:2 * GROUP_W]
        else:
            abb = ab.astype(BF16)
            for s in range(TM // seq_len):
                srows = slice(s * seq_len, (s + 1) * seq_len)
                orows = slice(hf * TM + s * seq_len, hf * TM + (s + 1) * seq_len)
                yc_ref[orows, :] = (_dot(dc_ref[...], abb[srows, 0:GROUP_W])
                                    + _dot(ds_ref[...], abb[srows, GROUP_W:2 * GROUP_W])).astype(BF16)
        cq = _dot(hb, wcq_ref[...])
        if lat:
            ckv = _dot(hb, wckv_ref[...])
            kr = _dot(hb, wkr_ref[...])
        else:
            kv = _dot(hb, wkv_ref[...])
            ckv = kv[:, 0:KV_LORA]
            kr = kv[:, KV_LORA:KV_LORA + LANES]
        qn = _rms_scale(cq, Q_LORA) * gq_ref[...]
        q4 = _dot(qn.astype(BF16), wq_ref[...])
        ckv_n = _rms_scale(ckv, KV_LORA) * gkv_ref[...]
        cb = ckv_n.astype(BF16)
        kk = _dot(cb, wk_ref[...])
        vt = _dot_nt(wvt_ref[...], cb).astype(BF16)
        if lat:
            vt_ref[:, rows] = vt
            q4 = _rope(q4, cos_ref[rows, :], sin_ref[rows, :])
            kr = _rope(kr, cos_ref[rows, :], sin_ref[rows, :])
        else:
            for s in range(TM // seq_len):
                vt_ref[hf * (TM // seq_len) + s] = vt[:, s * seq_len:(s + 1) * seq_len]
            ckv_ref[rows, :] = ckv_n
            kr_ref[rows, :] = kr[:, 0:QK_ROPE]
        q_ref[rows, :] = (q4 * Q_SCALE).astype(BF16)
        for hd in range(MLA_HEADS):
            cols = slice(hd * HEAD_PAD, (hd + 1) * HEAD_PAD)
            k_ref[rows, cols] = (kk[:, cols] + kr).astype(BF16)


def _premix(x, mod_l, lw, layer, lat, seq_len, tabs):
    t = x.shape[0]
    nt = t // PRE_TM
    nb = t // seq_len
    tpb = max(seq_len // PRE_TM, 1)
    row = lambda i: (i, 0)
    if lat:
        mod_map = lambda i: (1 + i // tpb, 0, 0)
        vt_spec = pl.BlockSpec((None, VW, PRE_TM), lambda i: (i // tpb, 0, i % tpb))
        tab_specs = [pl.BlockSpec((PRE_TM, LANES), lambda i: (i % tpb, 0))] * 2
        fourier_specs = [pl.BlockSpec((PRE_TM, GROUP_W), row)] * 2
        fourier_shapes = [jax.ShapeDtypeStruct((t, GROUP_W), F32)] * 2
    else:
        mod_map = lambda i: (0, 0, 0)
        vt_spec = pl.BlockSpec((PRE_TM // seq_len, VW, seq_len), lambda i: (i, 0, 0))
        tab_specs = [pl.BlockSpec((seq_len, seq_len), lambda i: (0, 0))] * 2
        fourier_specs = [pl.BlockSpec((PRE_TM, GROUP_W), row)]
        fourier_shapes = [jax.ShapeDtypeStruct((t, GROUP_W), BF16)]
    kv_names = ["w_ckv", "w_kr"] if lat else ["w_kv"]
    names = ["g_pre_mix", "w_main", "w_cq"] + kv_names + ["bd", "g_q", "wq4", "g_kv", "wk4", "wvt"]
    in_specs = [pl.BlockSpec((PRE_TM, D_MODEL), row), pl.BlockSpec((None, 6, D_MODEL), mod_map)]
    in_specs += [_layer_spec(lw[n], layer) for n in names] + tab_specs
    args = [x, mod_l] + [lw[n] for n in names] + list(tabs)
    out_specs = [pl.BlockSpec((PRE_TM, 512), row), pl.BlockSpec((PRE_TM, 768), row)] + fourier_specs + [
        pl.BlockSpec((PRE_TM, QW), row),
        pl.BlockSpec((PRE_TM, QW), row),
        vt_spec,
    ]
    out_shape = [jax.ShapeDtypeStruct((t, 512), BF16), jax.ShapeDtypeStruct((t, 768), BF16)] + fourier_shapes + [
        jax.ShapeDtypeStruct((t, QW), BF16),
        jax.ShapeDtypeStruct((t, QW), BF16),
        jax.ShapeDtypeStruct((nb, VW, seq_len), BF16),
    ]
    if not lat:
        out_specs += [pl.BlockSpec((PRE_TM, KV_LORA), row), pl.BlockSpec((PRE_TM, QK_ROPE), row)]
        out_shape += [jax.ShapeDtypeStruct((t, KV_LORA), F32), jax.ShapeDtypeStruct((t, QK_ROPE), F32)]
    return pl.pallas_call(
        functools.partial(_premix_kernel, lat=lat, seq_len=seq_len),
        grid=(nt,),
        in_specs=in_specs,
        out_specs=out_specs,
        out_shape=out_shape,
        scratch_shapes=[pltpu.VMEM((PRE_TM, D_MODEL), BF16)],
        compiler_params=_cparams(("arbitrary",)),
        name="premix_lat" if lat else "premix_ctx",
    )(*args)


def _seqmix_chunks(pa_ref, pb_ref, prev_ref, next_ref, ws_ref, sb_ref, cw_ref, cb_ref, o_ref, chunks, seq_len):
    i = pl.program_id(0)
    n_chunks = POST_TM // CHUNK
    grp = lax.broadcasted_iota(jnp.int32, (CHUNK, GROUP_W), 1) // (GROUP_W // A_GROUPS)
    r = lax.broadcasted_iota(jnp.int32, (CHUNK, GROUP_W), 0)

    def z_row(ref, row):
        blk = slice(row // 16 * 16, row // 16 * 16 + 16)
        zz = ref[blk, 2 * GROUP_W:3 * GROUP_W].astype(F32) * ref[blk, 0:GROUP_W].astype(F32)
        return zz[row % 16:row % 16 + 1]

    tile_pos = (i * POST_TM) & (seq_len - 1)
    for c in chunks:
        rows = slice(c * CHUNK, (c + 1) * CHUNK)
        u = pa_ref[rows, 0:GROUP_W].astype(F32)
        v = pa_ref[rows, GROUP_W:2 * GROUP_W]
        m4 = _dot(ws_ref[...], v)
        mixed = m4[0:CHUNK]
        for g in range(1, A_GROUPS):
            mixed = jnp.where(grp == g, m4[g * CHUNK:(g + 1) * CHUNK], mixed)
        o_ref[rows, 0:GROUP_W] = (u * (mixed + sb_ref[...])).astype(BF16)
        z = pb_ref[rows, 2 * GROUP_W:3 * GROUP_W].astype(F32) * pb_ref[rows, 0:GROUP_W].astype(F32)
        if c > 0:
            before = z_row(pb_ref, c * CHUNK - 1) if (c * CHUNK) % seq_len else jnp.zeros((1, GROUP_W), F32)
        else:
            before = jnp.where(tile_pos != 0, z_row(prev_ref, 15), 0.0)
        if c < n_chunks - 1:
            after = z_row(pb_ref, (c + 1) * CHUNK) if ((c + 1) * CHUNK) % seq_len else jnp.zeros((1, GROUP_W), F32)
        else:
            after = jnp.where(((tile_pos + POST_TM) & (seq_len - 1)) != 0, z_row(next_ref, 0), 0.0)
        z_prev = jnp.where(r == 0, before, pltpu.roll(z, 1, axis=0))
        z_next = jnp.where(r == CHUNK - 1, after, pltpu.roll(z, CHUNK - 1, axis=0))
        y = z_prev * cw_ref[0:1, :] + z * cw_ref[1:2, :] + z_next * cw_ref[2:3, :] + cb_ref[...]
        gb = pb_ref[rows, GROUP_W:2 * GROUP_W].astype(F32)
        o_ref[rows, GROUP_W:2 * GROUP_W] = (gb * y).astype(BF16)


def _posdft_kernel(a_ref, b_ref, twc_ref, tws_ref, dc_ref, ds_ref, perm_ref, o_ref, x_scr, y_scr, res_scr, *,
                   levels):
    n = a_ref.shape[0]
    tw0 = 0
    for lv in range(levels):
        size = n >> lv
        half = size // 2
        src_x, src_y = (a_ref, b_ref) if lv == 0 else (x_scr, y_scr)
        for base in range(0, n, size):
            for j0 in range(0, half, DFT_RB):
                top = slice(base + j0, base + j0 + DFT_RB)
                bot = slice(base + half + j0, base + half + j0 + DFT_RB)
                xt, xb, yt, yb = src_x[top, :], src_x[bot, :], src_y[top, :], src_y[bot, :]
                c = twc_ref[tw0 + j0:tw0 + j0 + DFT_RB, :]
                s = tws_ref[tw0 + j0:tw0 + j0 + DFT_RB, :]
                c = jnp.concatenate([c] * (GROUP_W // LANES), axis=1)
                s = jnp.concatenate([s] * (GROUP_W // LANES), axis=1)
                x_scr[top, :] = xt + xb
                y_scr[top, :] = yt + yb
                dx, dy = xt - xb, yt - yb
                x_scr[bot, :] = dx * c - dy * s
                y_scr[bot, :] = dx * s + dy * c
        tw0 += half
    sub = n >> levels
    nres = 1 << levels
    for p in range(nres):
        r = int(format(p, "0%db" % levels)[::-1], 2) if levels else 0
        rows = slice(p * sub, (p + 1) * sub)
        xb = x_scr[rows, :].astype(BF16)
        yb = y_scr[rows, :].astype(BF16)
        res_scr[r] = (_dot(dc_ref[...], xb) + _dot(ds_ref[...], yb)).astype(BF16)
    grp = sub // nres
    for g in range(nres):
        stack = jnp.concatenate([res_scr[r, g * grp:(g + 1) * grp, :] for r in range(nres)], axis=0)
        o_ref[g * sub:(g + 1) * sub, :] = _dot(perm_ref[...], stack).astype(BF16)


def _posdft(a, b, tabs, nb, n):
    twc, tws, dc, ds, perm = tabs
    sub = dc.shape[0]
    nres = n // sub
    levels = nres.bit_length() - 1
    const = lambda i: (0, 0)
    return pl.pallas_call(
        functools.partial(_posdft_kernel, levels=levels),
        grid=(nb,),
        in_specs=[
            pl.BlockSpec((n, GROUP_W), lambda i: (i, 0)),
            pl.BlockSpec((n, GROUP_W), lambda i: (i, 0)),
            pl.BlockSpec(twc.shape, const),
            pl.BlockSpec(tws.shape, const),
            pl.BlockSpec((sub, sub), const),
            pl.BlockSpec((sub, sub), const),
            pl.BlockSpec((sub, sub), const),
        ],
        out_specs=pl.BlockSpec((n, GROUP_W), lambda i: (i, 0)),
        out_shape=jax.ShapeDtypeStruct((nb * n, GROUP_W), BF16),
        scratch_shapes=[pltpu.VMEM((n, GROUP_W), F32), pltpu.VMEM((n, GROUP_W), F32),
                        pltpu.VMEM((nres, sub, GROUP_W), BF16)],
        compiler_params=_cparams(("arbitrary",)),
        name="posdft",
    )(a, b, twc, tws, dc, ds, perm)


def _attn_kernel(*refs, has_cache):
    if has_cache:
        q_ref, k_ref, vt_ref, kc_ref, vct_ref, o_ref, st_scr, pt_scr = refs
    else:
        q_ref, k_ref, vt_ref, o_ref, st_scr, pt_scr = refs
    nk = k_ref.shape[0]
    kc = min(nk, ATT_KC)
    segs = [(k_ref, vt_ref, c * kc, c * kc, kc) for c in range(nk // kc)]
    if has_cache:
        segs.append((kc_ref, vct_ref, 0, nk, kc_ref.shape[0]))
    nsub = q_ref.shape[0] // ATT_SUB
    items = [(s, h) for s in range(nsub) for h in range(MLA_HEADS)]
    outs = [[None] * MLA_HEADS for _ in range(nsub)]
    m_prev = None
    for t in range(len(items) + 1):
        if t < len(items):
            s, h = items[t]
            cols = slice(h * HEAD_PAD, (h + 1) * HEAD_PAD)
            qh = q_ref[s * ATT_SUB:(s + 1) * ATT_SUB, cols]
            m = None
        if t >= 1:
            ps, ph = items[t - 1]
            vrows = slice(ph * V_DIM, (ph + 1) * V_DIM)
            l = jnp.zeros((1, ATT_SUB), F32)
            ot = jnp.zeros((V_DIM, ATT_SUB), F32)
        for ci, (kref, vref, r0, s0, n) in enumerate(segs):
            if t < len(items):
                st = _dot_nt(kref[r0:r0 + n, cols], qh)
                st_scr[t % 2, s0:s0 + n, :] = st
                for j in range(0, n, ATT_EB):
                    cm = jnp.max(st[j:j + ATT_EB], axis=0, keepdims=True)
                    m = cm if m is None else jnp.maximum(m, cm)
            if t >= 1:
                for j in range(0, n, ATT_EB):
                    p = jnp.exp2(st_scr[(t - 1) % 2, s0 + j:s0 + j + ATT_EB, :] - m_prev)
                    l = l + jnp.sum(p, axis=0, keepdims=True)
                    pt_scr[ci % 2, j:j + ATT_EB, :] = p.astype(BF16)
                ot = ot + _dot(vref[vrows, r0:r0 + n], pt_scr[ci % 2, 0:n, :])
        if t >= 1:
            outs[ps][ph] = ot / l
        if t < len(items):
            m_prev = m
    for s in range(nsub):
        o_ref[s * ATT_SUB:(s + 1) * ATT_SUB, :] = jnp.concatenate(outs[s], axis=0).T.astype(BF16)


def _attention(q, k, vt, nb, seq_len, cache=None, layer=0):
    tq = min(seq_len, TQ)
    nk_total = seq_len + (cache[0].shape[2] if cache is not None else 0)
    q3 = q.reshape(nb, seq_len, QW)
    k3 = k.reshape(nb, seq_len, QW)
    in_specs = [
        pl.BlockSpec((None, tq, QW), lambda b, i: (b, i, 0)),
        pl.BlockSpec((None, seq_len, QW), lambda b, i: (b, 0, 0)),
        pl.BlockSpec((None, VW, seq_len), lambda b, i: (b, 0, 0)),
    ]
    args = [q3, k3, vt]
    if cache is not None:
        kc, vct = cache
        past = kc.shape[2]
        in_specs += [
            pl.BlockSpec((None, None, past, QW), lambda b, i: (b, layer, 0, 0)),
            pl.BlockSpec((None, None, VW, past), lambda b, i: (b, layer, 0, 0)),
        ]
        args += [kc, vct]
    out = pl.pallas_call(
        functools.partial(_attn_kernel, has_cache=cache is not None),
        grid=(nb, seq_len // tq),
        in_specs=in_specs,
        out_specs=pl.BlockSpec((None, tq, VW), lambda b, i: (b, i, 0)),
        out_shape=jax.ShapeDtypeStruct((nb, seq_len, VW), BF16),
        scratch_shapes=[pltpu.VMEM((2, nk_total, ATT_SUB), F32),
                        pltpu.VMEM((2, min(seq_len, ATT_KC), ATT_SUB), BF16)],
        compiler_params=_cparams(("arbitrary", "arbitrary")),
        name="attn_lat" if cache is not None else "attn_ctx",
    )(*args)
    return out.reshape(nb * seq_len, VW)


def _post_kernel(x_ref, mod_ref, pa_ref, pb_ref, prev_ref, next_ref, yc_ref, yd_ref, ws_ref, sb_ref, cw_ref, cb_ref,
                 wo_ref, gpm_ref, gpf_ref, gpo_ref, wgu_ref, wd_ref, o_ref, yab_scr, hb_scr, acc_scr, *, seq_len):
    halves = [slice(hf * TM, (hf + 1) * TM) for hf in range(POST_TM // TM)]

    def prologue(hf, rows):
        _seqmix_chunks(pa_ref, pb_ref, prev_ref, next_ref, ws_ref, sb_ref, cw_ref, cb_ref, yab_scr,
                       range(hf * (TM // CHUNK), (hf + 1) * (TM // CHUNK)), seq_len)
        mix = (_dot(yab_scr[rows, :], wo_ref[0:2 * GROUP_W, :])
               + _dot(yc_ref[rows, :], wo_ref[2 * GROUP_W:3 * GROUP_W, :])
               + _dot(yd_ref[rows, :], wo_ref[3 * GROUP_W:4 * GROUP_W, :]))
        x1 = x_ref[rows, :] + _rms_scale(mix, D_MODEL) * (mod_ref[2:3, :] * gpm_ref[...])
        o_ref[rows, :] = x1
        h2 = _rms_scale(x1, D_MODEL) * (gpf_ref[...] * (1.0 + mod_ref[4:5, :])) + mod_ref[3:4, :]
        hb_scr[rows, :] = h2.astype(BF16)

    def swiglu(rows):
        hb = hb_scr[rows, :]
        for c in range(N_FF_CHUNKS):
            g = _dot(hb, wgu_ref[:, c * FF_CHUNK:(c + 1) * FF_CHUNK])
            u = _dot(hb, wgu_ref[:, FF_HIDDEN + c * FF_CHUNK:FF_HIDDEN + (c + 1) * FF_CHUNK])
            act = (_silu(g) * u).astype(BF16)
            d = _dot(act, wd_ref[c * FF_CHUNK:(c + 1) * FF_CHUNK, :])
            if c == 0:
                acc_scr[rows, :] = d
            else:
                acc_scr[rows, :] += d

    def epilogue(rows):
        o_ref[rows, :] = o_ref[rows, :] + _rms_scale(acc_scr[rows, :], D_MODEL) * (mod_ref[5:6, :] * gpo_ref[...])

    prologue(0, halves[0])
    for hf, rows in enumerate(halves):
        if hf + 1 < len(halves):
            prologue(hf + 1, halves[hf + 1])
        if hf > 0:
            epilogue(halves[hf - 1])
        swiglu(rows)
    epilogue(halves[-1])


def _post(x, mod_l, pa, pb, yc, yd, lw, layer, lat, seq_len):
    t = x.shape[0]
    nt = t // POST_TM
    tpb = max(seq_len // POST_TM, 1)
    mod_map = (lambda i: (1 + i // tpb, 0, 0)) if lat else (lambda i: (0, 0, 0))
    row = lambda i: (i, 0)
    hb = POST_TM // 16
    last = t // 16 - 1
    resident = dict(pipeline_mode=pl.Buffered(1))
    mix_names = ["ws", "sb", "conv_w", "conv_b"]
    return pl.pallas_call(
        functools.partial(_post_kernel, seq_len=seq_len),
        grid=(nt,),
        in_specs=[
            pl.BlockSpec((POST_TM, D_MODEL), row),
            pl.BlockSpec((None, 6, D_MODEL), mod_map),
            pl.BlockSpec((POST_TM, 512), row),
            pl.BlockSpec((POST_TM, 768), row),
            pl.BlockSpec((16, 768), lambda i: (jnp.maximum(i * hb - 1, 0), 0)),
            pl.BlockSpec((16, 768), lambda i: (jnp.minimum((i + 1) * hb, last), 0)),
            pl.BlockSpec((POST_TM, GROUP_W), row),
            pl.BlockSpec((POST_TM, GROUP_W), row),
        ] + [_layer_spec(lw[n], layer) for n in mix_names] + [
            _layer_spec(lw["w_out"], layer, **resident),
            _layer_spec(lw["g_post_mix"], layer),
            _layer_spec(lw["g_pre_ffn"], layer),
            _layer_spec(lw["g_post_ffn"], layer),
            _layer_spec(lw["w_gate_up"], layer, **resident),
            _layer_spec(lw["w_down"], layer, **resident),
        ],
        out_specs=pl.BlockSpec((POST_TM, D_MODEL), row),
        out_shape=jax.ShapeDtypeStruct((t, D_MODEL), F32),
        scratch_shapes=[pltpu.VMEM((POST_TM, 2 * GROUP_W), BF16), pltpu.VMEM((POST_TM, D_MODEL), BF16),
                        pltpu.VMEM((POST_TM, D_MODEL), F32)],
        compiler_params=_cparams(("arbitrary",)),
        name="post_lat" if lat else "post_ctx",
    )(x, mod_l, pa, pb, pb, pb, yc, yd, *[lw[n] for n in mix_names], lw["w_out"], lw["g_post_mix"],
      lw["g_pre_ffn"], lw["g_post_ffn"], lw["w_gate_up"], lw["w_down"])


def _dft_tables(n, seq_len):
    scale = seq_len ** -0.5
    k = jnp.arange(n, dtype=jnp.int32)
    ang = ((k[:, None] * k[None, :]) % n).astype(F32) * (2.0 * math.pi / n)
    return (jnp.cos(ang) * scale).astype(BF16), (-jnp.sin(ang) * scale).astype(BF16)


def _twiddle_tables(n, sub):
    angs = []
    size = n
    while size > sub:
        angs.append(jnp.arange(size // 2, dtype=F32) * (2.0 * math.pi / size))
        size //= 2
    ang = jnp.concatenate(angs)[:, None]
    return jnp.broadcast_to(jnp.cos(ang), (ang.shape[0], LANES)), jnp.broadcast_to(jnp.sin(ang), (ang.shape[0], LANES))


def _interleave_perm(sub, nres):
    row = jnp.arange(sub, dtype=jnp.int32)
    src = (row % nres) * (sub // nres) + row // nres
    return (src[:, None] == row[None, :]).astype(BF16)


def _channel_dft():
    c = jnp.arange(C_CH, dtype=jnp.int32)
    ang = ((c[:, None] * c[None, :]) % C_CH).astype(F32) * (2.0 * math.pi / C_CH)
    eye = jnp.eye(C_GROUPS, dtype=F32)
    bc = jnp.kron(eye, jnp.cos(ang) * C_CH ** -0.5)
    bs = jnp.kron(eye, jnp.sin(ang) * C_CH ** -0.5)
    return jnp.concatenate([bc, bs], axis=1).astype(BF16)


def _rope_tables(n):
    pos = jnp.arange(n, dtype=jnp.int32)
    row = (pos // GRID_W).astype(F32)
    col = (pos % GRID_W).astype(F32)
    inv = ROPE_BASE ** (-jnp.arange(0, AXIS_ROPE, 2, dtype=F32) / AXIS_ROPE)
    half = AXIS_ROPE // 2
    ang = jnp.concatenate([row[:, None] * inv, row[:, None] * inv, col[:, None] * inv, col[:, None] * inv], axis=1)
    sign = jnp.tile(jnp.concatenate([-jnp.ones((half,), F32), jnp.ones((half,), F32)]), 2)
    cos = jnp.concatenate([jnp.cos(ang), jnp.ones((n, LANES - QK_ROPE), F32)], axis=1)
    sin = jnp.concatenate([jnp.sin(ang) * sign, jnp.zeros((n, LANES - QK_ROPE), F32)], axis=1)
    return cos, sin


def _layout_weights(g_pre_mix, g_post_mix, g_pre_ffn, g_post_ffn, w_in, spat_w, spat_b, conv_w, conv_b,
                    g_q_lora, w_uq, g_kv_lora, w_ukv, w_out, w_gate_up, w_down):
    d = DEPTH
    zeros = lambda *s: jnp.zeros(s, F32)
    c_kv0 = P_MAIN + Q_LORA
    lane_pad = lambda w, width: jnp.pad(w, ((0, 0), (0, 0), (0, width - w.shape[2]))).astype(BF16)
    uq = w_uq.reshape(d, Q_LORA, MLA_HEADS, QK_NOPE + QK_ROPE)
    wq4 = jnp.concatenate([uq[..., QK_NOPE:], uq[..., :QK_NOPE],
                           zeros(d, Q_LORA, MLA_HEADS, HEAD_PAD - QK_NOPE - QK_ROPE)], axis=3)
    wq4 = wq4.reshape(d, Q_LORA, QW)
    wq4 = jnp.concatenate([wq4, zeros(d, Q_LORA_PAD - Q_LORA, QW)], axis=1).astype(BF16)
    ukv = w_ukv.reshape(d, KV_LORA, MLA_HEADS, QK_NOPE + V_DIM)
    wk4 = jnp.concatenate([zeros(d, KV_LORA, MLA_HEADS, QK_ROPE), ukv[..., :QK_NOPE],
                           zeros(d, KV_LORA, MLA_HEADS, HEAD_PAD - QK_NOPE - QK_ROPE)], axis=3)
    wk4 = wk4.reshape(d, KV_LORA, QW).astype(BF16)
    wvt = ukv[..., QK_NOPE:].reshape(d, KV_LORA, VW).transpose(0, 2, 1).astype(BF16)
    g_q = jnp.concatenate([g_q_lora, zeros(d, Q_LORA_PAD - Q_LORA)], axis=1)
    return dict(
        g_pre_mix=g_pre_mix.reshape(d, 1, D_MODEL), g_post_mix=g_post_mix.reshape(d, 1, D_MODEL),
        g_pre_ffn=g_pre_ffn.reshape(d, 1, D_MODEL), g_post_ffn=g_post_ffn.reshape(d, 1, D_MODEL),
        w_main=w_in.astype(BF16),
        w_cq=lane_pad(w_in[:, :, P_MAIN:c_kv0], Q_LORA_PAD),
        w_kv=lane_pad(w_in[:, :, c_kv0:], KV_LORA + LANES),
        w_ckv=w_in[:, :, c_kv0:c_kv0 + KV_LORA].astype(BF16),
        w_kr=lane_pad(w_in[:, :, c_kv0 + KV_LORA:], LANES),
        wq4=wq4, wk4=wk4, wvt=wvt,
        g_q=g_q.reshape(d, 1, Q_LORA_PAD), g_kv=g_kv_lora.reshape(d, 1, KV_LORA),
        ws=spat_w.reshape(d, A_GROUPS * CHUNK, CHUNK).astype(BF16),
        sb=jnp.repeat(spat_b.transpose(0, 2, 1), GROUP_W // A_GROUPS, axis=2),
        conv_w=conv_w, conv_b=conv_b.reshape(d, 1, GROUP_W),
        w_out=w_out.astype(BF16), w_gate_up=w_gate_up.astype(BF16), w_down=w_down.astype(BF16),
        bd=jnp.broadcast_to(_channel_dft(), (d, GROUP_W, 2 * GROUP_W)),
    )


def _trunk_layer(x, mod_l, lw, layer, lat, nb, seq_len, premix_tabs, dft_tabs, cache):
    outs = _premix(x, mod_l, lw, layer, lat, seq_len, premix_tabs)
    if lat:
        pa, pb, a, b, q, k, vt = outs
        yc = _posdft(a, b, dft_tabs, nb, seq_len)
        states = ()
    else:
        pa, pb, yc, q, k, vt = outs[:6]
        states = outs[6:]
    yd = _attention(q, k, vt, nb, seq_len, cache, layer)
    x = _post(x, mod_l, pa, pb, yc, yd, lw, layer, lat, seq_len)
    return x, states


def kernel(x_prompt, x_sample, cache_ckv, cache_krope, c, c_ctx, w_ada, b_ada, g_pre_mix, g_post_mix, g_pre_ffn,
           g_post_ffn, w_in, spat_w, spat_b, conv_w, conv_b, g_q_lora, w_uq, g_kv_lora, w_ukv, w_out, w_gate_up,
           w_down):
    batch, seq, _ = x_prompt.shape
    dec_batch, dec_seq, _ = x_sample.shape
    assert (batch * seq) % POST_TM == 0 and TM % seq == 0 and dec_seq % POST_TM == 0 and dec_batch + 1 <= 8

    stacked = _layout_weights(g_pre_mix, g_post_mix, g_pre_ffn, g_post_ffn, w_in, spat_w, spat_b, conv_w, conv_b,
                              g_q_lora, w_uq, g_kv_lora, w_ukv, w_out, w_gate_up, w_down)
    dft_ctx = _dft_tables(seq, seq)
    dft_sub = min(dec_seq, DFT_SUB)
    dft_lat = (_twiddle_tables(dec_seq, dft_sub) + _dft_tables(dft_sub, dec_seq)
               + (_interleave_perm(dft_sub, dec_seq // dft_sub),))
    rope_tabs = _rope_tables(dec_seq)

    cond8 = jnp.concatenate([c_ctx[None, :], c, jnp.zeros((8 - 1 - dec_batch, D_MODEL), F32)], axis=0)
    mod = _modulation(cond8, w_ada, b_ada).reshape(DEPTH, 8, 6, D_MODEL)

    cache_kr_pad = jnp.pad(cache_krope, ((0, 0), (0, 0), (0, 0), (0, LANES - QK_ROPE)))
    cache = _cache_kv(cache_ckv, cache_kr_pad, stacked["wk4"], stacked["wvt"])

    y_ctx = x_prompt.reshape(batch * seq, D_MODEL)
    y_lat = x_sample.reshape(dec_batch * dec_seq, D_MODEL)
    ckv_list, krope_list = [], []
    for l in range(DEPTH):
        y_ctx, (ckv_l, kr_l) = _trunk_layer(y_ctx, mod[l], stacked, l, False, batch, seq, dft_ctx, None, None)
        ckv_list.append(ckv_l.reshape(batch, seq, KV_LORA))
        krope_list.append(kr_l.reshape(batch, seq, QK_ROPE))
        y_lat, _ = _trunk_layer(y_lat, mod[l], stacked, l, True, dec_batch, dec_seq, rope_tabs, dft_lat, cache)
    state_ckv = jnp.stack(ckv_list, axis=1)
    state_krope = jnp.stack(krope_list, axis=1)
    return (y_ctx.reshape(batch, seq, D_MODEL), y_lat.reshape(dec_batch, dec_seq, D_MODEL), state_ckv, state_krope)
```

```python
import functools
import math

import jax
import jax.numpy as jnp
from jax import lax
from jax.experimental import pallas as pl
from jax.experimental.pallas import tpu as pltpu

F32 = jnp.float32
BF16 = jnp.bfloat16

D_MODEL = 1024
DEPTH = 4
GRID_W = 64
GROUP_W = 256
A_GROUPS = 4
CHUNK = 128
C_GROUPS = 4
C_CH = 64
MLA_HEADS = 4
QK_NOPE = 64
QK_ROPE = 32
V_DIM = 64
Q_LORA = 192
KV_LORA = 128
ROPE_BASE = 10000.0
AXIS_ROPE = QK_ROPE // 2
FF_HIDDEN = 2816
EPS = 1e-6

LANES = 128
HEAD_PAD = 128
Q_LORA_PAD = 256
P_MAIN = 512 + 768 + 256
QW = MLA_HEADS * HEAD_PAD
VW = MLA_HEADS * V_DIM
FF_CHUNK = 256
N_FF_CHUNKS = FF_HIDDEN // FF_CHUNK
TM = 512
POST_TM = 1024
PRE_TM = 1024
TQ = 512
ATT_SUB = 256
ATT_KC = 2048
ATT_GROUP = 2
ATT_EB = 128
DFT_SUB = 512
DFT_RB = 32
VMEM_LIMIT = 56 * 1024 * 1024
Q_SCALE = (QK_NOPE + QK_ROPE) ** -0.5 * math.log2(math.e)


def _cparams(sem, **flags):
    return pltpu.CompilerParams(dimension_semantics=sem, vmem_limit_bytes=VMEM_LIMIT, flags=flags or None)


def _layer_spec(w, layer, **kw):
    tail = w.shape[1:]
    nz = (0,) * len(tail)
    return pl.BlockSpec((None,) + tail, lambda *_: (layer,) + nz, **kw)


def _rms_scale(x, n):
    ms = jnp.sum(x * x, axis=-1, keepdims=True) * (1.0 / n)
    return x * lax.rsqrt(ms + EPS)


def _silu(x):
    return x * jax.nn.sigmoid(x)


def _dot(a, b):
    return jnp.dot(a, b, preferred_element_type=F32)


def _dot_nt(a, b):
    return lax.dot_general(a, b, (((1,), (1,)), ((), ())), preferred_element_type=F32)


def _mod_kernel(cond_ref, w_ref, b_ref, o_ref):
    s = _silu(cond_ref[...])
    o_ref[...] = _dot(s.astype(BF16), w_ref[...].astype(BF16)) + b_ref[...]


def _modulation(cond8, w_ada, b_ada):
    nt = 6 * D_MODEL // 1024
    return pl.pallas_call(
        _mod_kernel,
        grid=(DEPTH, nt),
        in_specs=[
            pl.BlockSpec((8, D_MODEL), lambda l, j: (0, 0)),
            pl.BlockSpec((None, D_MODEL, 1024), lambda l, j: (l, 0, j)),
            pl.BlockSpec((None, 1, 1024), lambda l, j: (l, 0, j)),
        ],
        out_specs=pl.BlockSpec((None, 8, 1024), lambda l, j: (l, 0, j)),
        out_shape=jax.ShapeDtypeStruct((DEPTH, 8, 6 * D_MODEL), F32),
        compiler_params=_cparams(("arbitrary", "arbitrary")),
        name="modulation",
    )(cond8, w_ada, b_ada.reshape(DEPTH, 1, 6 * D_MODEL))


def _cachekv_kernel(ckv_ref, kr_ref, wk_ref, wvt_ref, k_ref, vt_ref):
    c = ckv_ref[...].astype(BF16)
    kk = _dot(c, wk_ref[...])
    kr = kr_ref[...]
    for h in range(MLA_HEADS):
        k_ref[:, h * HEAD_PAD:(h + 1) * HEAD_PAD] = (kk[:, h * HEAD_PAD:(h + 1) * HEAD_PAD] + kr).astype(BF16)
    vt_ref[...] = _dot_nt(wvt_ref[...], c).astype(BF16)


def _cache_kv(cache_ckv, cache_kr_pad, wk4, wvt):
    nb, _, past, _ = cache_ckv.shape
    return pl.pallas_call(
        _cachekv_kernel,
        grid=(nb, DEPTH),
        in_specs=[
            pl.BlockSpec((None, None, past, KV_LORA), lambda b, l: (b, l, 0, 0)),
            pl.BlockSpec((None, None, past, LANES), lambda b, l: (b, l, 0, 0)),
            pl.BlockSpec((None, KV_LORA, QW), lambda b, l: (l, 0, 0)),
            pl.BlockSpec((None, VW, KV_LORA), lambda b, l: (l, 0, 0)),
        ],
        out_specs=[
            pl.BlockSpec((None, None, past, QW), lambda b, l: (b, l, 0, 0)),
            pl.BlockSpec((None, None, VW, past), lambda b, l: (b, l, 0, 0)),
        ],
        out_shape=[
            jax.ShapeDtypeStruct((nb, DEPTH, past, QW), BF16),
            jax.ShapeDtypeStruct((nb, DEPTH, VW, past), BF16),
        ],
        compiler_params=_cparams(("arbitrary", "arbitrary")),
        name="cache_kv",
    )(cache_ckv, cache_kr_pad, wk4, wvt)


def _rope(x, cos, sin):
    w = x.shape[1]
    lane = lax.broadcasted_iota(jnp.int32, x.shape, 1)
    up = pltpu.roll(x, w - 8, axis=1)
    dn = pltpu.roll(x, 8, axis=1)
    sw = jnp.where((lane & 15) < 8, up, dn)
    reps = w // LANES
    if reps > 1:
        cos = jnp.concatenate([cos] * reps, axis=1)
        sin = jnp.concatenate([sin] * reps, axis=1)
    return x * cos + sw * sin


def _premix_kernel(*refs, lat, seq_len):
    if lat:
        (x_ref, mod_ref, g_ref, win_ref, wcq_ref, wckv_ref, wkr_ref, bd_ref, gq_ref, wq_ref, gkv_ref, wk_ref,
         wvt_ref, cos_ref, sin_ref, pa_ref, pb_ref, a_ref, b_ref, q_ref, k_ref, vt_ref) = refs[:-1]
    else:
        (x_ref, mod_ref, g_ref, win_ref, wcq_ref, wkv_ref, bd_ref, gq_ref, wq_ref, gkv_ref, wk_ref,
         wvt_ref, dc_ref, ds_ref, pa_ref, pb_ref, yc_ref, q_ref, k_ref, vt_ref, ckv_ref, kr_ref) = refs[:-1]
    hb_scr = refs[-1]
    halves = [slice(hf * TM, (hf + 1) * TM) for hf in range(PRE_TM // TM)]
    gain = g_ref[...] * (1.0 + mod_ref[1:2, :])
    for rows in halves:
        h = _rms_scale(x_ref[rows, :], D_MODEL) * gain + mod_ref[0:1, :]
        hb_scr[rows, :] = h.astype(BF16)
    for hf, rows in enumerate(halves):
        hb = hb_scr[rows, :]
        pa_ref[rows, :] = _dot(hb, win_ref[:, 0:512]).astype(BF16)
        pb_ref[rows, :] = _dot(hb, win_ref[:, 512:1280]).astype(BF16)
        pc = _dot(hb, win_ref[:, 1280:P_MAIN]).astype(BF16)
        ab = _dot(pc, bd_ref[...])
        if lat:
            a_ref[rows, :] = ab[:, 0:GROUP_W]
            b_ref[rows, :] = ab[:, GROUP_W:2 * GROUP_W]
        else:
            abb = ab.astype(BF16)
            for s in range(TM // seq_len):
                srows = slice(s * seq_len, (s + 1) * seq_len)
                orows = slice(hf * TM + s * seq_len, hf * TM + (s + 1) * seq_len)
                yc_ref[orows, :] = (_dot(dc_ref[...], abb[srows, 0:GROUP_W])
                                    + _dot(ds_ref[...], abb[srows, GROUP_W:2 * GROUP_W])).astype(BF16)
        cq = _dot(hb, wcq_ref[...])
        if lat:
            ckv = _dot(hb, wckv_ref[...])
            kr = _dot(hb, wkr_ref[...])
        else:
            kv = _dot(hb, wkv_ref[...])
            ckv = kv[:, 0:KV_LORA]
            kr = kv[:, KV_LORA:KV_LORA + LANES]
        qn = _rms_scale(cq, Q_LORA) * gq_ref[...]
        q4 = _dot(qn.astype(BF16), wq_ref[...])
        ckv_n = _rms_scale(ckv, KV_LORA) * gkv_ref[...]
        cb = ckv_n.astype(BF16)
        kk = _dot(cb, wk_ref[...])
        vt = _dot_nt(wvt_ref[...], cb).astype(BF16)
        if lat:
            vt_ref[:, rows] = vt
            q4 = _rope(q4, cos_ref[rows, :], sin_ref[rows, :])
            kr = _rope(kr, cos_ref[rows, :], sin_ref[rows, :])
        else:
            for s in range(TM // seq_len):
                vt_ref[hf * (TM // seq_len) + s] = vt[:, s * seq_len:(s + 1) * seq_len]
            ckv_ref[rows, :] = ckv_n
            kr_ref[rows, :] = kr[:, 0:QK_ROPE]
        q_ref[rows, :] = (q4 * Q_SCALE).astype(BF16)
        for hd in range(MLA_HEADS):
            cols = slice(hd * HEAD_PAD, (hd + 1) * HEAD_PAD)
            k_ref[rows, cols] = (kk[:, cols] + kr).astype(BF16)


def _premix(x, mod_l, lw, layer, lat, seq_len, tabs):
    t = x.shape[0]
    nt = t // PRE_TM
    nb = t // seq_len
    tpb = max(seq_len // PRE_TM, 1)
    row = lambda i: (i, 0)
    if lat:
        mod_map = lambda i: (1 + i // tpb, 0, 0)
        vt_spec = pl.BlockSpec((None, VW, PRE_TM), lambda i: (i // tpb, 0, i % tpb))
        tab_specs = [pl.BlockSpec((PRE_TM, LANES), lambda i: (i % tpb, 0))] * 2
        fourier_specs = [pl.BlockSpec((PRE_TM, GROUP_W), row)] * 2
        fourier_shapes = [jax.ShapeDtypeStruct((t, GROUP_W), F32)] * 2
    else:
        mod_map = lambda i: (0, 0, 0)
        vt_spec = pl.BlockSpec((PRE_TM // seq_len, VW, seq_len), lambda i: (i, 0, 0))
        tab_specs = [pl.BlockSpec((seq_len, seq_len), lambda i: (0, 0))] * 2
        fourier_specs = [pl.BlockSpec((PRE_TM, GROUP_W), row)]
        fourier_shapes = [jax.ShapeDtypeStruct((t, GROUP_W), BF16)]
    kv_names = ["w_ckv", "w_kr"] if lat else ["w_kv"]
    names = ["g_pre_mix", "w_main", "w_cq"] + kv_names + ["bd", "g_q", "wq4", "g_kv", "wk4", "wvt"]
    in_specs = [pl.BlockSpec((PRE_TM, D_MODEL), row), pl.BlockSpec((None, 6, D_MODEL), mod_map)]
    in_specs += [_layer_spec(lw[n], layer) for n in names] + tab_specs
    args = [x, mod_l] + [lw[n] for n in names] + list(tabs)
    out_specs = [pl.BlockSpec((PRE_TM, 512), row), pl.BlockSpec((PRE_TM, 768), row)] + fourier_specs + [
        pl.BlockSpec((PRE_TM, QW), row),
        pl.BlockSpec((PRE_TM, QW), row),
        vt_spec,
    ]
    out_shape = [jax.ShapeDtypeStruct((t, 512), BF16), jax.ShapeDtypeStruct((t, 768), BF16)] + fourier_shapes + [
        jax.ShapeDtypeStruct((t, QW), BF16),
        jax.ShapeDtypeStruct((t, QW), BF16),
        jax.ShapeDtypeStruct((nb, VW, seq_len), BF16),
    ]
    if not lat:
        out_specs += [pl.BlockSpec((PRE_TM, KV_LORA), row), pl.BlockSpec((PRE_TM, QK_ROPE), row)]
        out_shape += [jax.ShapeDtypeStruct((t, KV_LORA), F32), jax.ShapeDtypeStruct((t, QK_ROPE), F32)]
    return pl.pallas_call(
        functools.partial(_premix_kernel, lat=lat, seq_len=seq_len),
        grid=(nt,),
        in_specs=in_specs,
        out_specs=out_specs,
        out_shape=out_shape,
        scratch_shapes=[pltpu.VMEM((PRE_TM, D_MODEL), BF16)],
        compiler_params=_cparams(("arbitrary",)),
        name="premix_lat" if lat else "premix_ctx",
    )(*args)


def _seqmix_chunks(pa_ref, pb_ref, prev_ref, next_ref, ws_ref, sb_ref, cw_ref, cb_ref, o_ref, chunks, seq_len):
    i = pl.program_id(0)
    n_chunks = POST_TM // CHUNK
    grp = lax.broadcasted_iota(jnp.int32, (CHUNK, GROUP_W), 1) // (GROUP_W // A_GROUPS)
    r = lax.broadcasted_iota(jnp.int32, (CHUNK, GROUP_W), 0)

    def z_row(ref, row):
        blk = slice(row // 16 * 16, row // 16 * 16 + 16)
        zz = ref[blk, 2 * GROUP_W:3 * GROUP_W].astype(F32) * ref[blk, 0:GROUP_W].astype(F32)
        return zz[row % 16:row % 16 + 1]

    tile_pos = (i * POST_TM) & (seq_len - 1)
    for c in chunks:
        rows = slice(c * CHUNK, (c + 1) * CHUNK)
        u = pa_ref[rows, 0:GROUP_W].astype(F32)
        v = pa_ref[rows, GROUP_W:2 * GROUP_W]
        m4 = _dot(ws_ref[...], v)
        mixed = m4[0:CHUNK]
        for g in range(1, A_GROUPS):
            mixed = jnp.where(grp == g, m4[g * CHUNK:(g + 1) * CHUNK], mixed)
        o_ref[rows, 0:GROUP_W] = (u * (mixed + sb_ref[...])).astype(BF16)
        z = pb_ref[rows, 2 * GROUP_W:3 * GROUP_W].astype(F32) * pb_ref[rows, 0:GROUP_W].astype(F32)
        if c > 0:
            before = z_row(pb_ref, c * CHUNK - 1) if (c * CHUNK) % seq_len else jnp.zeros((1, GROUP_W), F32)
        else:
            before = jnp.where(tile_pos != 0, z_row(prev_ref, 15), 0.0)
        if c < n_chunks - 1:
            after = z_row(pb_ref, (c + 1) * CHUNK) if ((c + 1) * CHUNK) % seq_len else jnp.zeros((1, GROUP_W), F32)
        else:
            after = jnp.where(((tile_pos + POST_TM) & (seq_len - 1)) != 0, z_row(next_ref, 0), 0.0)
        z_prev = jnp.where(r == 0, before, pltpu.roll(z, 1, axis=0))
        z_next = jnp.where(r == CHUNK - 1, after, pltpu.roll(z, CHUNK - 1, axis=0))
        y = z_prev * cw_ref[0:1, :] + z * cw_ref[1:2, :] + z_next * cw_ref[2:3, :] + cb_ref[...]
        gb = pb_ref[rows, GROUP_W:2 * GROUP_W].astype(F32)
        o_ref[rows, GROUP_W:2 * GROUP_W] = (gb * y).astype(BF16)


def _posdft_kernel(a_ref, b_ref, twc_ref, tws_ref, dc_ref, ds_ref, perm_ref, o_ref, x_scr, y_scr, res_scr, *,
                   levels):
    n = a_ref.shape[0]
    tw0 = 0
    for lv in range(levels):
        size = n >> lv
        half = size // 2
        src_x, src_y = (a_ref, b_ref) if lv == 0 else (x_scr, y_scr)
        for base in range(0, n, size):
            for j0 in range(0, half, DFT_RB):
                top = slice(base + j0, base + j0 + DFT_RB)
                bot = slice(base + half + j0, base + half + j0 + DFT_RB)
                xt, xb, yt, yb = src_x[top, :], src_x[bot, :], src_y[top, :], src_y[bot, :]
                c = twc_ref[tw0 + j0:tw0 + j0 + DFT_RB, :]
                s = tws_ref[tw0 + j0:tw0 + j0 + DFT_RB, :]
                c = jnp.concatenate([c] * (GROUP_W // LANES), axis=1)
                s = jnp.concatenate([s] * (GROUP_W // LANES), axis=1)
                x_scr[top, :] = xt + xb
                y_scr[top, :] = yt + yb
                dx, dy = xt - xb, yt - yb
                x_scr[bot, :] = dx * c - dy * s
                y_scr[bot, :] = dx * s + dy * c
        tw0 += half
    sub = n >> levels
    nres = 1 << levels
    for p in range(nres):
        r = int(format(p, "0%db" % levels)[::-1], 2) if levels else 0
        rows = slice(p * sub, (p + 1) * sub)
        xb = x_scr[rows, :].astype(BF16)
        yb = y_scr[rows, :].astype(BF16)
        res_scr[r] = (_dot(dc_ref[...], xb) + _dot(ds_ref[...], yb)).astype(BF16)
    grp = sub // nres
    for g in range(nres):
        stack = jnp.concatenate([res_scr[r, g * grp:(g + 1) * grp, :] for r in range(nres)], axis=0)
        o_ref[g * sub:(g + 1) * sub, :] = _dot(perm_ref[...], stack).astype(BF16)


def _posdft(a, b, tabs, nb, n):
    twc, tws, dc, ds, perm = tabs
    sub = dc.shape[0]
    nres = n // sub
    levels = nres.bit_length() - 1
    const = lambda i: (0, 0)
    return pl.pallas_call(
        functools.partial(_posdft_kernel, levels=levels),
        grid=(nb,),
        in_specs=[
            pl.BlockSpec((n, GROUP_W), lambda i: (i, 0)),
            pl.BlockSpec((n, GROUP_W), lambda i: (i, 0)),
            pl.BlockSpec(twc.shape, const),
            pl.BlockSpec(tws.shape, const),
            pl.BlockSpec((sub, sub), const),
            pl.BlockSpec((sub, sub), const),
            pl.BlockSpec((sub, sub), const),
        ],
        out_specs=pl.BlockSpec((n, GROUP_W), lambda i: (i, 0)),
        out_shape=jax.ShapeDtypeStruct((nb * n, GROUP_W), BF16),
        scratch_shapes=[pltpu.VMEM((n, GROUP_W), F32), pltpu.VMEM((n, GROUP_W), F32),
                        pltpu.VMEM((nres, sub, GROUP_W), BF16)],
        compiler_params=_cparams(("arbitrary",)),
        name="posdft",
    )(a, b, twc, tws, dc, ds, perm)


def _attn_kernel(*refs, has_cache):
    if has_cache:
        q_ref, k_ref, vt_ref, kc_ref, vct_ref, o_ref, st_scr, pt_scr = refs
    else:
        q_ref, k_ref, vt_ref, o_ref, st_scr, pt_scr = refs
    grouped = len(k_ref.shape) == 3
    nk = k_ref.shape[-2]
    kc = min(nk, ATT_KC)

    def segments(s):
        kr, vr = (k_ref.at[s], vt_ref.at[s]) if grouped else (k_ref, vt_ref)
        sg = [(kr, vr, c * kc, c * kc, kc) for c in range(nk // kc)]
        if has_cache:
            sg.append((kc_ref, vct_ref, 0, nk, kc_ref.shape[0]))
        return sg

    nsub = q_ref.shape[0] // ATT_SUB
    items = [(s, h) for s in range(nsub) for h in range(MLA_HEADS)]
    outs = [[None] * MLA_HEADS for _ in range(nsub)]
    m_prev = None
    for t in range(len(items) + 1):
        if t < len(items):
            s, h = items[t]
            cols = slice(h * HEAD_PAD, (h + 1) * HEAD_PAD)
            qh = q_ref[s * ATT_SUB:(s + 1) * ATT_SUB, cols]
            m = None
            segs_q = segments(s)
        if t >= 1:
            ps, ph = items[t - 1]
            vrows = slice(ph * V_DIM, (ph + 1) * V_DIM)
            l = jnp.zeros((1, ATT_SUB), F32)
            ot = jnp.zeros((V_DIM, ATT_SUB), F32)
            segs_p = segments(ps)
        for ci in range(len(segments(0))):
            if t < len(items):
                kref, _, r0, s0, n = segs_q[ci]
                st = _dot_nt(kref[r0:r0 + n, cols], qh)
                st_scr[t % 2, s0:s0 + n, :] = st
                for j in range(0, n, ATT_EB):
                    cm = jnp.max(st[j:j + ATT_EB], axis=0, keepdims=True)
                    m = cm if m is None else jnp.maximum(m, cm)
            if t >= 1:
                _, vref, r0, s0, n = segs_p[ci]
                for j in range(0, n, ATT_EB):
                    p = jnp.exp2(st_scr[(t - 1) % 2, s0 + j:s0 + j + ATT_EB, :] - m_prev)
                    l = l + jnp.sum(p, axis=0, keepdims=True)
                    pt_scr[ci % 2, j:j + ATT_EB, :] = p.astype(BF16)
                ot = ot + _dot(vref[vrows, r0:r0 + n], pt_scr[ci % 2, 0:n, :])
        if t >= 1:
            outs[ps][ph] = ot / l
        if t < len(items):
            m_prev = m
    for s in range(nsub):
        o_ref[s * ATT_SUB:(s + 1) * ATT_SUB, :] = jnp.concatenate(outs[s], axis=0).T.astype(BF16)


def _attention_grouped(q, k, vt, nb, seq_len):
    g = ATT_GROUP
    assert seq_len == ATT_SUB and nb % g == 0
    out = pl.pallas_call(
        functools.partial(_attn_kernel, has_cache=False),
        grid=(nb // g,),
        in_specs=[
            pl.BlockSpec((None, g * seq_len, QW), lambda b: (b, 0, 0)),
            pl.BlockSpec((g, seq_len, QW), lambda b: (b, 0, 0)),
            pl.BlockSpec((g, VW, seq_len), lambda b: (b, 0, 0)),
        ],
        out_specs=pl.BlockSpec((None, g * seq_len, VW), lambda b: (b, 0, 0)),
        out_shape=jax.ShapeDtypeStruct((nb // g, g * seq_len, VW), BF16),
        scratch_shapes=[pltpu.VMEM((2, seq_len, ATT_SUB), F32), pltpu.VMEM((2, seq_len, ATT_SUB), BF16)],
        compiler_params=_cparams(("arbitrary",)),
        name="attn_ctx",
    )(q.reshape(nb // g, g * seq_len, QW), k.reshape(nb, seq_len, QW), vt)
    return out.reshape(nb * seq_len, VW)


def _attention(q, k, vt, nb, seq_len, cache=None, layer=0):
    if cache is None and seq_len == ATT_SUB and nb % ATT_GROUP == 0:
        return _attention_grouped(q, k, vt, nb, seq_len)
    tq = min(seq_len, TQ)
    nk_total = seq_len + (cache[0].shape[2] if cache is not None else 0)
    q3 = q.reshape(nb, seq_len, QW)
    k3 = k.reshape(nb, seq_len, QW)
    in_specs = [
        pl.BlockSpec((None, tq, QW), lambda b, i: (b, i, 0)),
        pl.BlockSpec((None, seq_len, QW), lambda b, i: (b, 0, 0)),
        pl.BlockSpec((None, VW, seq_len), lambda b, i: (b, 0, 0)),
    ]
    args = [q3, k3, vt]
    if cache is not None:
        kc, vct = cache
        past = kc.shape[2]
        in_specs += [
            pl.BlockSpec((None, None, past, QW), lambda b, i: (b, layer, 0, 0)),
            pl.BlockSpec((None, None, VW, past), lambda b, i: (b, layer, 0, 0)),
        ]
        args += [kc, vct]
    out = pl.pallas_call(
        functools.partial(_attn_kernel, has_cache=cache is not None),
        grid=(nb, seq_len // tq),
        in_specs=in_specs,
        out_specs=pl.BlockSpec((None, tq, VW), lambda b, i: (b, i, 0)),
        out_shape=jax.ShapeDtypeStruct((nb, seq_len, VW), BF16),
        scratch_shapes=[pltpu.VMEM((2, nk_total, ATT_SUB), F32),
                        pltpu.VMEM((2, min(seq_len, ATT_KC), ATT_SUB), BF16)],
        compiler_params=_cparams(("arbitrary", "arbitrary")),
        name="attn_lat" if cache is not None else "attn_ctx",
    )(*args)
    return out.reshape(nb * seq_len, VW)


def _post_kernel(x_ref, mod_ref, pa_ref, pb_ref, prev_ref, next_ref, yc_ref, yd_ref, ws_ref, sb_ref, cw_ref, cb_ref,
                 wo_ref, gpm_ref, gpf_ref, gpo_ref, wgu_ref, wd_ref, o_ref, yab_scr, hb_scr, acc_scr, *, seq_len):
    halves = [slice(hf * TM, (hf + 1) * TM) for hf in range(POST_TM // TM)]

    def prologue(hf, rows):
        _seqmix_chunks(pa_ref, pb_ref, prev_ref, next_ref, ws_ref, sb_ref, cw_ref, cb_ref, yab_scr,
                       range(hf * (TM // CHUNK), (hf + 1) * (TM // CHUNK)), seq_len)
        mix = (_dot(yab_scr[rows, :], wo_ref[0:2 * GROUP_W, :])
               + _dot(yc_ref[rows, :], wo_ref[2 * GROUP_W:3 * GROUP_W, :])
               + _dot(yd_ref[rows, :], wo_ref[3 * GROUP_W:4 * GROUP_W, :]))
        x1 = x_ref[rows, :] + _rms_scale(mix, D_MODEL) * (mod_ref[2:3, :] * gpm_ref[...])
        o_ref[rows, :] = x1
        h2 = _rms_scale(x1, D_MODEL) * (gpf_ref[...] * (1.0 + mod_ref[4:5, :])) + mod_ref[3:4, :]
        hb_scr[rows, :] = h2.astype(BF16)

    def swiglu(rows):
        hb = hb_scr[rows, :]
        for c in range(N_FF_CHUNKS):
            g = _dot(hb, wgu_ref[:, c * FF_CHUNK:(c + 1) * FF_CHUNK])
            u = _dot(hb, wgu_ref[:, FF_HIDDEN + c * FF_CHUNK:FF_HIDDEN + (c + 1) * FF_CHUNK])
            act = (_silu(g) * u).astype(BF16)
            d = _dot(act, wd_ref[c * FF_CHUNK:(c + 1) * FF_CHUNK, :])
            if c == 0:
                acc_scr[rows, :] = d
            else:
                acc_scr[rows, :] += d

    def epilogue(rows):
        o_ref[rows, :] = o_ref[rows, :] + _rms_scale(acc_scr[rows, :], D_MODEL) * (mod_ref[5:6, :] * gpo_ref[...])

    prologue(0, halves[0])
    for hf, rows in enumerate(halves):
        if hf + 1 < len(halves):
            prologue(hf + 1, halves[hf + 1])
        if hf > 0:
            epilogue(halves[hf - 1])
        swiglu(rows)
    epilogue(halves[-1])


def _post(x, mod_l, pa, pb, yc, yd, lw, layer, lat, seq_len):
    t = x.shape[0]
    nt = t // POST_TM
    tpb = max(seq_len // POST_TM, 1)
    mod_map = (lambda i: (1 + i // tpb, 0, 0)) if lat else (lambda i: (0, 0, 0))
    row = lambda i: (i, 0)
    hb = POST_TM // 16
    last = t // 16 - 1
    resident = dict(pipeline_mode=pl.Buffered(1))
    mix_names = ["ws", "sb", "conv_w", "conv_b"]
    return pl.pallas_call(
        functools.partial(_post_kernel, seq_len=seq_len),
        grid=(nt,),
        in_specs=[
            pl.BlockSpec((POST_TM, D_MODEL), row),
            pl.BlockSpec((None, 6, D_MODEL), mod_map),
            pl.BlockSpec((POST_TM, 512), row),
            pl.BlockSpec((POST_TM, 768), row),
            pl.BlockSpec((16, 768), lambda i: (jnp.maximum(i * hb - 1, 0), 0)),
            pl.BlockSpec((16, 768), lambda i: (jnp.minimum((i + 1) * hb, last), 0)),
            pl.BlockSpec((POST_TM, GROUP_W), row),
            pl.BlockSpec((POST_TM, GROUP_W), row),
        ] + [_layer_spec(lw[n], layer) for n in mix_names] + [
            _layer_spec(lw["w_out"], layer, **resident),
            _layer_spec(lw["g_post_mix"], layer),
            _layer_spec(lw["g_pre_ffn"], layer),
            _layer_spec(lw["g_post_ffn"], layer),
            _layer_spec(lw["w_gate_up"], layer, **resident),
            _layer_spec(lw["w_down"], layer, **resident),
        ],
        out_specs=pl.BlockSpec((POST_TM, D_MODEL), row),
        out_shape=jax.ShapeDtypeStruct((t, D_MODEL), F32),
        scratch_shapes=[pltpu.VMEM((POST_TM, 2 * GROUP_W), BF16), pltpu.VMEM((POST_TM, D_MODEL), BF16),
                        pltpu.VMEM((POST_TM, D_MODEL), F32)],
        compiler_params=_cparams(("arbitrary",)),
        name="post_lat" if lat else "post_ctx",
    )(x, mod_l, pa, pb, pb, pb, yc, yd, *[lw[n] for n in mix_names], lw["w_out"], lw["g_post_mix"],
      lw["g_pre_ffn"], lw["g_post_ffn"], lw["w_gate_up"], lw["w_down"])


def _dft_tables(n, seq_len):
    scale = seq_len ** -0.5
    k = jnp.arange(n, dtype=jnp.int32)
    ang = ((k[:, None] * k[None, :]) % n).astype(F32) * (2.0 * math.pi / n)
    return (jnp.cos(ang) * scale).astype(BF16), (-jnp.sin(ang) * scale).astype(BF16)


def _twiddle_tables(n, sub):
    angs = []
    size = n
    while size > sub:
        angs.append(jnp.arange(size // 2, dtype=F32) * (2.0 * math.pi / size))
        size //= 2
    ang = jnp.concatenate(angs)[:, None]
    return jnp.broadcast_to(jnp.cos(ang), (ang.shape[0], LANES)), jnp.broadcast_to(jnp.sin(ang), (ang.shape[0], LANES))


def _interleave_perm(sub, nres):
    row = jnp.arange(sub, dtype=jnp.int32)
    src = (row % nres) * (sub // nres) + row // nres
    return (src[:, None] == row[None, :]).astype(BF16)


def _channel_dft():
    c = jnp.arange(C_CH, dtype=jnp.int32)
    ang = ((c[:, None] * c[None, :]) % C_CH).astype(F32) * (2.0 * math.pi / C_CH)
    eye = jnp.eye(C_GROUPS, dtype=F32)
    bc = jnp.kron(eye, jnp.cos(ang) * C_CH ** -0.5)
    bs = jnp.kron(eye, jnp.sin(ang) * C_CH ** -0.5)
    return jnp.concatenate([bc, bs], axis=1).astype(BF16)


def _rope_tables(n):
    pos = jnp.arange(n, dtype=jnp.int32)
    row = (pos // GRID_W).astype(F32)
    col = (pos % GRID_W).astype(F32)
    inv = ROPE_BASE ** (-jnp.arange(0, AXIS_ROPE, 2, dtype=F32) / AXIS_ROPE)
    half = AXIS_ROPE // 2
    ang = jnp.concatenate([row[:, None] * inv, row[:, None] * inv, col[:, None] * inv, col[:, None] * inv], axis=1)
    sign = jnp.tile(jnp.concatenate([-jnp.ones((half,), F32), jnp.ones((half,), F32)]), 2)
    cos = jnp.concatenate([jnp.cos(ang), jnp.ones((n, LANES - QK_ROPE), F32)], axis=1)
    sin = jnp.concatenate([jnp.sin(ang) * sign, jnp.zeros((n, LANES - QK_ROPE), F32)], axis=1)
    return cos, sin


def _layout_weights(g_pre_mix, g_post_mix, g_pre_ffn, g_post_ffn, w_in, spat_w, spat_b, conv_w, conv_b,
                    g_q_lora, w_uq, g_kv_lora, w_ukv, w_out, w_gate_up, w_down):
    d = DEPTH
    zeros = lambda *s: jnp.zeros(s, F32)
    c_kv0 = P_MAIN + Q_LORA
    lane_pad = lambda w, width: jnp.pad(w, ((0, 0), (0, 0), (0, width - w.shape[2]))).astype(BF16)
    uq = w_uq.reshape(d, Q_LORA, MLA_HEADS, QK_NOPE + QK_ROPE)
    wq4 = jnp.concatenate([uq[..., QK_NOPE:], uq[..., :QK_NOPE],
                           zeros(d, Q_LORA, MLA_HEADS, HEAD_PAD - QK_NOPE - QK_ROPE)], axis=3)
    wq4 = wq4.reshape(d, Q_LORA, QW)
    wq4 = jnp.concatenate([wq4, zeros(d, Q_LORA_PAD - Q_LORA, QW)], axis=1).astype(BF16)
    ukv = w_ukv.reshape(d, KV_LORA, MLA_HEADS, QK_NOPE + V_DIM)
    wk4 = jnp.concatenate([zeros(d, KV_LORA, MLA_HEADS, QK_ROPE), ukv[..., :QK_NOPE],
                           zeros(d, KV_LORA, MLA_HEADS, HEAD_PAD - QK_NOPE - QK_ROPE)], axis=3)
    wk4 = wk4.reshape(d, KV_LORA, QW).astype(BF16)
    wvt = ukv[..., QK_NOPE:].reshape(d, KV_LORA, VW).transpose(0, 2, 1).astype(BF16)
    g_q = jnp.concatenate([g_q_lora, zeros(d, Q_LORA_PAD - Q_LORA)], axis=1)
    return dict(
        g_pre_mix=g_pre_mix.reshape(d, 1, D_MODEL), g_post_mix=g_post_mix.reshape(d, 1, D_MODEL),
        g_pre_ffn=g_pre_ffn.reshape(d, 1, D_MODEL), g_post_ffn=g_post_ffn.reshape(d, 1, D_MODEL),
        w_main=w_in.astype(BF16),
        w_cq=lane_pad(w_in[:, :, P_MAIN:c_kv0], Q_LORA_PAD),
        w_kv=lane_pad(w_in[:, :, c_kv0:], KV_LORA + LANES),
        w_ckv=w_in[:, :, c_kv0:c_kv0 + KV_LORA].astype(BF16),
        w_kr=lane_pad(w_in[:, :, c_kv0 + KV_LORA:], LANES),
        wq4=wq4, wk4=wk4, wvt=wvt,
        g_q=g_q.reshape(d, 1, Q_LORA_PAD), g_kv=g_kv_lora.reshape(d, 1, KV_LORA),
        ws=spat_w.reshape(d, A_GROUPS * CHUNK, CHUNK).astype(BF16),
        sb=jnp.repeat(spat_b.transpose(0, 2, 1), GROUP_W // A_GROUPS, axis=2),
        conv_w=conv_w, conv_b=conv_b.reshape(d, 1, GROUP_W),
        w_out=w_out.astype(BF16), w_gate_up=w_gate_up.astype(BF16), w_down=w_down.astype(BF16),
        bd=jnp.broadcast_to(_channel_dft(), (d, GROUP_W, 2 * GROUP_W)),
    )


def _trunk_layer(x, mod_l, lw, layer, lat, nb, seq_len, premix_tabs, dft_tabs, cache):
    outs = _premix(x, mod_l, lw, layer, lat, seq_len, premix_tabs)
    if lat:
        pa, pb, a, b, q, k, vt = outs
        yc = _posdft(a, b, dft_tabs, nb, seq_len)
        states = ()
    else:
        pa, pb, yc, q, k, vt = outs[:6]
        states = outs[6:]
    yd = _attention(q, k, vt, nb, seq_len, cache, layer)
    x = _post(x, mod_l, pa, pb, yc, yd, lw, layer, lat, seq_len)
    return x, states


def kernel(x_prompt, x_sample, cache_ckv, cache_krope, c, c_ctx, w_ada, b_ada, g_pre_mix, g_post_mix, g_pre_ffn,
           g_post_ffn, w_in, spat_w, spat_b, conv_w, conv_b, g_q_lora, w_uq, g_kv_lora, w_ukv, w_out, w_gate_up,
           w_down):
    batch, seq, _ = x_prompt.shape
    dec_batch, dec_seq, _ = x_sample.shape
    assert (batch * seq) % POST_TM == 0 and TM % seq == 0 and dec_seq % POST_TM == 0 and dec_batch + 1 <= 8

    stacked = _layout_weights(g_pre_mix, g_post_mix, g_pre_ffn, g_post_ffn, w_in, spat_w, spat_b, conv_w, conv_b,
                              g_q_lora, w_uq, g_kv_lora, w_ukv, w_out, w_gate_up, w_down)
    dft_ctx = _dft_tables(seq, seq)
    dft_sub = min(dec_seq, DFT_SUB)
    dft_lat = (_twiddle_tables(dec_seq, dft_sub) + _dft_tables(dft_sub, dec_seq)
               + (_interleave_perm(dft_sub, dec_seq // dft_sub),))
    rope_tabs = _rope_tables(dec_seq)

    cond8 = jnp.concatenate([c_ctx[None, :], c, jnp.zeros((8 - 1 - dec_batch, D_MODEL), F32)], axis=0)
    mod = _modulation(cond8, w_ada, b_ada).reshape(DEPTH, 8, 6, D_MODEL)

    cache_kr_pad = jnp.pad(cache_krope, ((0, 0), (0, 0), (0, 0), (0, LANES - QK_ROPE)))
    cache = _cache_kv(cache_ckv, cache_kr_pad, stacked["wk4"], stacked["wvt"])

    y_ctx = x_prompt.reshape(batch * seq, D_MODEL)
    y_lat = x_sample.reshape(dec_batch * dec_seq, D_MODEL)
    ckv_list, krope_list = [], []
    for l in range(DEPTH):
        y_ctx, (ckv_l, kr_l) = _trunk_layer(y_ctx, mod[l], stacked, l, False, batch, seq, dft_ctx, None, None)
        ckv_list.append(ckv_l.reshape(batch, seq, KV_LORA))
        krope_list.append(kr_l.reshape(batch, seq, QK_ROPE))
        y_lat, _ = _trunk_layer(y_lat, mod[l], stacked, l, True, dec_batch, dec_seq, rope_tabs, dft_lat, cache)
    state_ckv = jnp.stack(ckv_list, axis=1)
    state_krope = jnp.stack(krope_list, axis=1)
    return (y_ctx.reshape(batch, seq, D_MODEL), y_lat.reshape(dec_batch, dec_seq, D_MODEL), state_ckv, state_krope)
```
